```python
import math
import jax, jax.numpy as jnp
from jax import lax
import numpy as np

D_MODEL = 4096
BATCH = 4
SEQ = 2048
DEPTH = 4
DEC_BATCH = 8
DEC_SEQ = 1
PAST_LEN = 8192
PAGE_SIZE = 128

N_META = 16
HEAD_DIM = 128
CHUNK = 64
Q_BLOCK = 128
A_HEADS = (3 * D_MODEL // 8) // HEAD_DIM
A_DK = HEAD_DIM
A_DV = HEAD_DIM
A_WIDTH = A_HEADS * A_DV
CONV_WIDTH = 4
B_QK_DIM = HEAD_DIM
B_V_DIM = 2 * HEAD_DIM
B_HEADS = (D_MODEL // 4) // B_V_DIM
B_WIDTH = B_HEADS * B_V_DIM
C_HEADS = 4
C_WIDTH = D_MODEL - A_WIDTH - B_WIDTH
C_DV = C_WIDTH // C_HEADS
C_DK = C_DV // 2
C_GATE_RANK = 16
C_GATE_TAU = 16.0
D_FF = ((8 * D_MODEL + 3 * 256 - 1) // (3 * 256)) * 256
EPS = 1e-6

IN_SIZES = (3 * A_WIDTH, A_WIDTH, A_HEADS, A_HEADS,
            2 * B_HEADS * B_QK_DIM, 2 * B_HEADS * B_QK_DIM, B_WIDTH,
            C_HEADS * C_DK, C_HEADS * C_DK, C_WIDTH, C_WIDTH, C_GATE_RANK)
IN_WIDTH = sum(IN_SIZES)

kernel_name = "hymba_deltanet_diffattn_gla_step"


def rms_norm(x, g):
    xf = x.astype(jnp.float32)
    y = xf * lax.rsqrt(jnp.mean(xf * xf, axis=-1, keepdims=True) + EPS)
    return (y * g.astype(jnp.float32)).astype(x.dtype)


def l2_norm(x):
    return x * lax.rsqrt(jnp.sum(x * x, axis=-1, keepdims=True) + EPS)


def split_columns(p):
    idx, acc = [], 0
    for s in IN_SIZES[:-1]:
        acc += s
        idx.append(acc)
    return jnp.split(p, idx, axis=-1)


def causal_conv(u, buf, w):
    ext = jnp.concatenate([buf.astype(u.dtype), u], axis=1)
    out = lax.conv_general_dilated(ext, w.astype(u.dtype)[:, None, :], window_strides=(1,), padding='VALID',
                                   dimension_numbers=('NWC', 'WIO', 'NWC'), feature_group_count=u.shape[-1])
    return out, ext[:, -(CONV_WIDTH - 1):]


def gdn_chunk(S, q, k, v, g, beta):
    C = q.shape[1]
    G = jnp.moveaxis(jnp.cumsum(g, axis=1), 1, -1)
    bt = jnp.moveaxis(beta, 1, -1)
    qh, kh, vh = (jnp.moveaxis(a, 1, 2) for a in (q, k, v))
    incl = jnp.tril(jnp.ones((C, C), dtype=bool))
    strict = jnp.tril(jnp.ones((C, C), dtype=bool), -1)
    decay = jnp.exp(jnp.where(incl, G[..., :, None] - G[..., None, :], -jnp.inf))
    kk = jnp.einsum('bhid,bhjd->bhij', kh, kh)
    lower = jnp.where(strict, bt[..., :, None] * kk * decay, 0.0) + jnp.eye(C, dtype=kk.dtype)
    rhs = jnp.concatenate([bt[..., None] * vh, (bt * jnp.exp(G))[..., None] * kh], axis=-1)
    sol = lax.linalg.triangular_solve(lower, rhs, left_side=True, lower=True, unit_diagonal=True)
    u = sol[..., :A_DV] - jnp.einsum('bhck,bhkv->bhcv', sol[..., A_DV:], S)
    qk = jnp.einsum('bhid,bhjd->bhij', qh, kh) * decay
    o = jnp.einsum('bhck,bhkv->bhcv', qh * jnp.exp(G)[..., None], S) + jnp.einsum('bhij,bhjv->bhiv', qk, u)
    G_last = G[..., -1:]
    S_new = jnp.exp(G_last)[..., None] * S + jnp.einsum('bhck,bhcv->bhkv', kh * jnp.exp(G_last - G)[..., None], u)
    return S_new, jnp.moveaxis(o, 2, 1)


def gla_chunk(S, q, k, v, log_a):
    C = q.shape[1]
    Bc = jnp.cumsum(log_a, axis=1)
    incl = jnp.tril(jnp.ones((C, C), dtype=bool))[None, :, :, None, None]
    rel = jnp.exp(jnp.where(incl, Bc[:, :, None] - Bc[:, None, :], -jnp.inf))
    att = jnp.einsum('bthd,btjhd,bjhd->bhtj', q, rel, k)
    o = jnp.einsum('bthd,bhde->bthe', q * jnp.exp(Bc), S) + jnp.einsum('bhtj,bjhe->bthe', att, v)
    B_last = Bc[:, -1]
    S_new = jnp.exp(B_last)[..., None] * S + jnp.einsum('bjhd,bjhe->bhde', k * jnp.exp(B_last[:, None] - Bc), v)
    return S_new, o


def run_chunks(chunk_fn, S0, xs, n_lead):
    if n_lead is None:
        return chunk_fn(S0, *xs)
    S, o_lead = chunk_fn(S0, *(a[:, :n_lead] for a in xs))
    b = xs[0].shape[0]
    T_rest = xs[0].shape[1] - n_lead
    n_chunks = T_rest // CHUNK

    def to_chunks(a):
        a = a[:, n_lead:]
        return jnp.moveaxis(a.reshape((b, n_chunks, CHUNK) + a.shape[2:]), 1, 0)

    S, o_rest = lax.scan(lambda s, c: chunk_fn(s, *c), S, tuple(to_chunks(a) for a in xs))
    o_rest = jnp.moveaxis(o_rest, 0, 1).reshape((b, T_rest) + o_rest.shape[3:])
    return S, jnp.concatenate([o_lead, o_rest], axis=1)


def diff_attention(q, k, v, lam, slopes):
    b, Tq = q.shape[:2]
    Tk = k.shape[1]
    bs = min(Q_BLOCK, Tq)
    nb = -(-Tq // bs)
    pad = nb * bs - Tq
    q_pos = (Tk - Tq) + jnp.arange(nb * bs, dtype=jnp.int32)
    k_pos = jnp.arange(Tk, dtype=jnp.int32)
    qb = jnp.pad(q, ((0, 0), (0, pad), (0, 0), (0, 0))).reshape(b, nb, bs, 2, B_HEADS, B_QK_DIM)
    qb = jnp.moveaxis(qb, 1, 0)
    kr = k.reshape(b, Tk, 2, B_HEADS, B_QK_DIM)

    def block(args):
        qi, pi = args
        s = jnp.einsum('bqmhd,bkmhd->bmhqk', qi, kr) * (B_QK_DIM ** -0.5)
        dist = pi[:, None] - k_pos[None, :]
        s = s - slopes[:, None, None] * dist.astype(s.dtype)
        s = jnp.where(dist >= 0, s, -jnp.inf)
        p = jax.nn.softmax(s, axis=-1)
        w = p[:, 0] - lam * p[:, 1]
        return jnp.einsum('bhqk,bkhe->bqhe', w, v)

    o = lax.map(block, (qb, q_pos.reshape(nb, bs)))
    return jnp.moveaxis(o, 0, 1).reshape(b, nb * bs, B_HEADS, B_V_DIM)[:, :Tq]


def layer_forward(h, conv_buf, S_a, S_c, past_k, past_v, n_lead, layer,
                  norm1_g, w_in, a_conv_w, a_log, a_dt_bias, a_norm_g,
                  b_lq1, b_lk1, b_lq2, b_lk2, b_norm_g,
                  c_gate_w2, c_gate_b, c_norm_g, w_o, norm2_g, w_gate, w_up, w_down):
    f32 = jnp.float32
    dt = h.dtype
    b, T, _ = h.shape
    u = rms_norm(h, norm1_g)
    (a_qkv, a_z, a_b, a_a, b_q, b_k, b_v, c_q, c_k, c_v, c_r, c_g) = split_columns(u @ w_in)

    a_c, new_conv = causal_conv(a_qkv, conv_buf, a_conv_w)
    a_c = jax.nn.silu(a_c.astype(f32))
    aq, ak, av = (t.reshape(b, T, A_HEADS, HEAD_DIM) for t in jnp.split(a_c, 3, axis=-1))
    aq = l2_norm(aq) * (A_DK ** -0.5)
    ak = l2_norm(ak)
    beta = jax.nn.sigmoid(a_b.astype(f32))
    g = -jnp.exp(a_log.astype(f32)) * jax.nn.softplus(a_a.astype(f32) + a_dt_bias.astype(f32))
    S_a_new, ao = run_chunks(gdn_chunk, S_a.astype(f32), (aq, ak, av, g, beta), n_lead)
    ao = rms_norm(ao, a_norm_g) * jax.nn.silu(a_z.astype(f32).reshape(b, T, A_HEADS, A_DV))

    bq = b_q.astype(f32).reshape(b, T, 2 * B_HEADS, B_QK_DIM)
    bk = b_k.reshape(b, T, 2 * B_HEADS, B_QK_DIM)
    bv = b_v.reshape(b, T, B_HEADS, B_V_DIM)
    if past_k is None:
        k_all, v_all = bk.astype(f32), bv.astype(f32)
    else:
        k_all = jnp.concatenate([past_k.astype(f32), bk.astype(f32)], axis=1)
        v_all = jnp.concatenate([past_v.astype(f32), bv.astype(f32)], axis=1)
    lam_init = 0.8 - 0.6 * math.exp(-0.3 * layer)
    lam = (jnp.exp(jnp.sum(b_lq1.astype(f32) * b_lk1.astype(f32)))
           - jnp.exp(jnp.sum(b_lq2.astype(f32) * b_lk2.astype(f32))) + lam_init)
    slopes = jnp.exp2(-8.0 * (jnp.arange(B_HEADS, dtype=f32) + 1.0) / B_HEADS)
    bo = diff_attention(bq, k_all, v_all, lam, slopes)
    bo = rms_norm(bo, b_norm_g) * (1.0 - lam_init)

    cq = c_q.astype(f32).reshape(b, T, C_HEADS, C_DK) * (C_DK ** -0.5)
    ck = c_k.astype(f32).reshape(b, T, C_HEADS, C_DK)
    cv = c_v.astype(f32).reshape(b, T, C_HEADS, C_DV)
    c_log_a = jax.nn.log_sigmoid(c_g.astype(f32) @ c_gate_w2.astype(f32) + c_gate_b.astype(f32)) / C_GATE_TAU
    c_log_a = c_log_a.reshape(b, T, C_HEADS, C_DK)
    S_c_new, co = run_chunks(gla_chunk, S_c.astype(f32), (cq, ck, cv, c_log_a), n_lead)
    co = rms_norm(co, c_norm_g) * jax.nn.silu(c_r.astype(f32).reshape(b, T, C_HEADS, C_DV))

    mix = jnp.concatenate([ao.reshape(b, T, A_WIDTH), bo.reshape(b, T, B_WIDTH), co.reshape(b, T, C_WIDTH)], axis=-1)
    h = h + mix.astype(dt) @ w_o
    u2 = rms_norm(h, norm2_g)
    h = h + (jax.nn.silu(u2 @ w_gate) * (u2 @ w_up)) @ w_down
    return h, new_conv, S_a_new.astype(dt), S_c_new.astype(dt), bk, bv


def setup_inputs(seed: int = 0) -> dict:
    key = jax.random.key(seed)
    ks = iter(jax.random.split(key, 40))
    f32 = jnp.float32

    def nrm(shape, scale=1.0):
        return jax.random.normal(next(ks), shape, f32) * scale

    n_pages = PAST_LEN // PAGE_SIZE
    n_pool = (DEC_BATCH * n_pages * 5) // 4
    page_table = jax.random.permutation(next(ks), n_pool)[:DEC_BATCH * n_pages]
    page_table = page_table.reshape(DEC_BATCH, n_pages).astype(jnp.int32)
    dt_init = jnp.exp(jax.random.uniform(next(ks), (DEPTH, A_HEADS), f32, math.log(1e-3), math.log(1e-1)))
    a_dt_bias = dt_init + jnp.log(-jnp.expm1(-dt_init))
    a_log = jnp.log(jax.random.uniform(next(ks), (DEPTH, A_HEADS), f32, 1.0, 16.0))
    return {
        "x_prompt": nrm((BATCH, SEQ, D_MODEL)),
        "x_sample": nrm((DEC_BATCH, DEC_SEQ, D_MODEL)),
        "cache_k": nrm((DEPTH, n_pool, PAGE_SIZE, 2 * B_HEADS, B_QK_DIM)),
        "cache_v": nrm((DEPTH, n_pool, PAGE_SIZE, B_HEADS, B_V_DIM)),
        "page_table": page_table,
        "state_a_conv": nrm((DEPTH, DEC_BATCH, CONV_WIDTH - 1, 3 * A_WIDTH)),
        "state_a_rec": nrm((DEPTH, DEC_BATCH, A_HEADS, A_DK, A_DV), 0.05),
        "state_c_rec": nrm((DEPTH, DEC_BATCH, C_HEADS, C_DK, C_DV), 0.05),
        "meta_tokens": nrm((N_META, D_MODEL)),
        "norm1_g": 1.0 + nrm((DEPTH, D_MODEL), 0.02),
        "w_in": nrm((DEPTH, D_MODEL, IN_WIDTH), D_MODEL ** -0.5),
        "a_conv_w": nrm((DEPTH, CONV_WIDTH, 3 * A_WIDTH), CONV_WIDTH ** -0.5),
        "a_log": a_log,
        "a_dt_bias": a_dt_bias,
        "a_norm_g": 1.0 + nrm((DEPTH, A_DV), 0.02),
        "b_lambda_q1": nrm((DEPTH, B_QK_DIM), 0.1),
        "b_lambda_k1": nrm((DEPTH, B_QK_DIM), 0.1),
        "b_lambda_q2": nrm((DEPTH, B_QK_DIM), 0.1),
        "b_lambda_k2": nrm((DEPTH, B_QK_DIM), 0.1),
        "b_norm_g": 1.0 + nrm((DEPTH, B_V_DIM), 0.02),
        "c_gate_w2": nrm((DEPTH, C_GATE_RANK, C_HEADS * C_DK), C_GATE_RANK ** -0.5),
        "c_gate_b": nrm((DEPTH, C_HEADS * C_DK), 0.01),
        "c_norm_g": 1.0 + nrm((DEPTH, C_DV), 0.02),
        "w_o": nrm((DEPTH, D_MODEL, D_MODEL), D_MODEL ** -0.5),
        "norm2_g": 1.0 + nrm((DEPTH, D_MODEL), 0.02),
        "w_gate": nrm((DEPTH, D_MODEL, D_FF), D_MODEL ** -0.5),
        "w_up": nrm((DEPTH, D_MODEL, D_FF), D_MODEL ** -0.5),
        "w_down": nrm((DEPTH, D_FF, D_MODEL), D_FF ** -0.5),
        "final_norm_g": 1.0 + nrm((D_MODEL,), 0.02),
    }


def reference(x_prompt, x_sample, cache_k, cache_v, page_table, state_a_conv, state_a_rec, state_c_rec,
              meta_tokens, norm1_g, w_in, a_conv_w, a_log, a_dt_bias, a_norm_g,
              b_lambda_q1, b_lambda_k1, b_lambda_q2, b_lambda_k2, b_norm_g,
              c_gate_w2, c_gate_b, c_norm_g, w_o, norm2_g, w_gate, w_up, w_down, final_norm_g):
    bp = x_prompt.shape[0]
    bsmp = x_sample.shape[0]
    dt = x_prompt.dtype
    n_pages = page_table.shape[1]
    hp = jnp.concatenate([jnp.broadcast_to(meta_tokens.astype(dt)[None], (bp, N_META, D_MODEL)), x_prompt], axis=1)
    hs = x_sample
    conv0 = jnp.zeros((bp, CONV_WIDTH - 1, 3 * A_WIDTH), dt)
    sa0 = jnp.zeros((bp, A_HEADS, A_DK, A_DV), jnp.float32)
    sc0 = jnp.zeros((bp, C_HEADS, C_DK, C_DV), jnp.float32)
    kp, vp, ksm, vsm, cvp, cvs, sap, sas, scp, scs = ([] for _ in range(10))
    for l in range(DEPTH):
        lp = (norm1_g[l], w_in[l], a_conv_w[l], a_log[l], a_dt_bias[l], a_norm_g[l],
              b_lambda_q1[l], b_lambda_k1[l], b_lambda_q2[l], b_lambda_k2[l], b_norm_g[l],
              c_gate_w2[l], c_gate_b[l], c_norm_g[l], w_o[l], norm2_g[l], w_gate[l], w_up[l], w_down[l])
        hp, c1, a1, s1, k1, v1 = layer_forward(hp, conv0, sa0, sc0, None, None, N_META, l, *lp)
        past_k = cache_k[l, page_table].reshape(bsmp, n_pages * PAGE_SIZE, 2 * B_HEADS, B_QK_DIM)
        past_v = cache_v[l, page_table].reshape(bsmp, n_pages * PAGE_SIZE, B_HEADS, B_V_DIM)
        hs, c2, a2, s2, k2, v2 = layer_forward(hs, state_a_conv[l], state_a_rec[l], state_c_rec[l],
                                               past_k, past_v, None, l, *lp)
        kp.append(k1); vp.append(v1); ksm.append(k2); vsm.append(v2)
        cvp.append(c1); cvs.append(c2); sap.append(a1); sas.append(a2); scp.append(s1); scs.append(s2)
    y_prompt = rms_norm(hp, final_norm_g)[:, N_META:]
    y_sample = rms_norm(hs, final_norm_g)
    return (y_prompt, y_sample, jnp.stack(kp), jnp.stack(vp), jnp.stack(ksm), jnp.stack(vsm),
            jnp.stack(cvp), jnp.stack(cvs), jnp.stack(sap), jnp.stack(sas), jnp.stack(scp), jnp.stack(scs))
```

```python
import functools
import math

import jax
import jax.numpy as jnp
from jax import lax
from jax.experimental import pallas as pl
from jax.experimental.pallas import tpu as pltpu

F32 = jnp.float32
BF16 = jnp.bfloat16
HIGHEST = lax.Precision.HIGHEST

D_MODEL = 4096
N_META = 16
DEPTH = 4
PAGE_SIZE = 128
CHUNK = 64
SUB = 16
A_HEADS, A_DK, A_DV, A_WIDTH, CONV_WIDTH = 12, 128, 128, 1536, 4
B_HEADS, B_QK_DIM, B_V_DIM, B_WIDTH = 4, 128, 256, 1024
C_HEADS, C_DK, C_DV, C_WIDTH, C_GATE_RANK, C_GATE_TAU = 4, 192, 384, 1536, 16, 16.0
D_FF = 11008
EPS = 1e-6
IN_SIZES = (3 * A_WIDTH, A_WIDTH, A_HEADS, A_HEADS, 2 * B_HEADS * B_QK_DIM, 2 * B_HEADS * B_QK_DIM, B_WIDTH,
            C_HEADS * C_DK, C_HEADS * C_DK, C_WIDTH, C_WIDTH, C_GATE_RANK)

P_QKV, P_Z, P_BQ, P_BK, P_BV, P_CQK, P_CV, P_CR, P_WIDTH = 0, 4608, 6144, 7168, 8192, 9216, 10752, 12288, 13824
SM_B, SM_A, SM_G, SM_WIDTH = 0, A_HEADS, 2 * A_HEADS, 128

VMEM_LIMIT_BYTES = 56 * 1024 * 1024


def _tile(n, cap, align):
    return max(t for t in range(align, min(n, cap) + 1, align) if n % t == 0)


def _params(*sem):
    return pltpu.CompilerParams(dimension_semantics=sem, vmem_limit_bytes=VMEM_LIMIT_BYTES)


def _sigmoid(x):
    return 1.0 / (1.0 + jnp.exp(-x))


def _silu(x):
    return x * _sigmoid(x)


def _softplus(x):
    return jnp.maximum(x, 0.0) + jnp.log1p(jnp.exp(-jnp.abs(x)))


def _mm(a, b):
    return jnp.dot(a.astype(BF16), b.astype(BF16), preferred_element_type=F32)


def _mm_nt(a, b):
    return lax.dot_general(a.astype(BF16), b.astype(BF16), (((1,), (1,)), ((), ())), preferred_element_type=F32)


def _mm_tn(a, b):
    return lax.dot_general(a.astype(BF16), b.astype(BF16), (((0,), (0,)), ((), ())), preferred_element_type=F32)


def _mm_hi(a, b):
    return jnp.dot(a, b, preferred_element_type=F32, precision=HIGHEST)


def _iota2(shape, dim):
    return lax.broadcasted_iota(jnp.int32, shape, dim)


def _rmsnorm_kernel(x_ref, g_ref, o_ref):
    x = x_ref[...]
    y = x * lax.rsqrt(jnp.mean(x * x, axis=-1, keepdims=True) + EPS)
    o_ref[...] = (y * g_ref[...]).astype(o_ref.dtype)


def rmsnorm(x, g, out_dtype, tm):
    m, d = x.shape
    return pl.pallas_call(
        _rmsnorm_kernel,
        grid=(m // tm,),
        in_specs=[pl.BlockSpec((tm, d), lambda i: (i, 0)), pl.BlockSpec((1, d), lambda i: (0, 0))],
        out_specs=pl.BlockSpec((tm, d), lambda i: (i, 0)),
        out_shape=jax.ShapeDtypeStruct((m, d), out_dtype),
        compiler_params=_params("parallel"),
        name="rmsnorm",
    )(x, g.reshape(1, d))


def _mm_kernel(x_ref, w_ref, o_ref):
    o_ref[...] = jnp.dot(x_ref[...], w_ref[...], preferred_element_type=F32).astype(o_ref.dtype)


def matmul(x, w, tm, tn, out_dtype=F32):
    m, k = x.shape
    n = w.shape[1]
    return pl.pallas_call(
        _mm_kernel,
        grid=(n // tn, m // tm),
        in_specs=[pl.BlockSpec((tm, k), lambda j, i: (i, 0)), pl.BlockSpec((k, tn), lambda j, i: (0, j))],
        out_specs=pl.BlockSpec((tm, tn), lambda j, i: (i, j)),
        out_shape=jax.ShapeDtypeStruct((m, n), out_dtype),
        compiler_params=_params("parallel", "parallel"),
        name="proj_in",
    )(x, w)


def _mm3_res_kernel(xa_ref, xb_ref, xc_ref, wa_ref, wb_ref, wc_ref, r_ref, o_ref):
    acc = jnp.dot(xa_ref[...], wa_ref[...], preferred_element_type=F32)
    acc += jnp.dot(xb_ref[...], wb_ref[...], preferred_element_type=F32)
    acc += jnp.dot(xc_ref[...], wc_ref[...], preferred_element_type=F32)
    o_ref[...] = r_ref[...] + acc


def out_proj(xa, xb, xc, wa, wb, wc, res, tm, tn):
    m = xa.shape[0]
    n = wa.shape[1]
    xs = lambda a: pl.BlockSpec((tm, a.shape[1]), lambda j, i: (i, 0))
    ws = lambda a: pl.BlockSpec((a.shape[0], tn), lambda j, i: (0, j))
    return pl.pallas_call(
        _mm3_res_kernel,
        grid=(n // tn, m // tm),
        in_specs=[xs(xa), xs(xb), xs(xc), ws(wa), ws(wb), ws(wc), pl.BlockSpec((tm, tn), lambda j, i: (i, j))],
        out_specs=pl.BlockSpec((tm, tn), lambda j, i: (i, j)),
        out_shape=jax.ShapeDtypeStruct((m, n), F32),
        compiler_params=_params("parallel", "parallel"),
        name="proj_out",
    )(xa, xb, xc, wa, wb, wc, res)


def _gateup_kernel(x_ref, wg_ref, wu_ref, o_ref):
    x = x_ref[...]
    g = jnp.dot(x, wg_ref[...], preferred_element_type=F32)
    u = jnp.dot(x, wu_ref[...], preferred_element_type=F32)
    o_ref[...] = (_silu(g) * u).astype(o_ref.dtype)


def ffn_gateup(x, wg, wu, tm, tn):
    m, k = x.shape
    n = wg.shape[1]
    return pl.pallas_call(
        _gateup_kernel,
        grid=(m // tm, n // tn),
        in_specs=[pl.BlockSpec((tm, k), lambda i, j: (i, 0)), pl.BlockSpec((k, tn), lambda i, j: (0, j)),
                  pl.BlockSpec((k, tn), lambda i, j: (0, j))],
        out_specs=pl.BlockSpec((tm, tn), lambda i, j: (i, j)),
        out_shape=jax.ShapeDtypeStruct((m, n), BF16),
        compiler_params=_params("parallel", "parallel"),
        name="ffn_gateup",
    )(x, wg, wu)


def _down_kernel(x_ref, w_ref, r_ref, o_ref):
    o_ref[...] = r_ref[...] + jnp.dot(x_ref[...], w_ref[...], preferred_element_type=F32)


def ffn_down(x, w, res, tm, tn):
    m, k = x.shape
    n = w.shape[1]
    return pl.pallas_call(
        _down_kernel,
        grid=(m // tm, n // tn),
        in_specs=[pl.BlockSpec((tm, k), lambda i, j: (i, 0)), pl.BlockSpec((k, tn), lambda i, j: (0, j)),
                  pl.BlockSpec((tm, tn), lambda i, j: (i, j))],
        out_specs=pl.BlockSpec((tm, tn), lambda i, j: (i, j)),
        out_shape=jax.ShapeDtypeStruct((m, n), F32),
        compiler_params=_params("parallel", "parallel"),
        name="ffn_down",
    )(x, w, res)


def _inv_unit_lower(a, c):
    r = _iota2((c, c), 0)
    col = _iota2((c, c), 1)
    eye = (r == col).astype(F32)
    bs = min(c, SUB)
    if c > bs:
        sh = int(math.log2(bs))
        same = (r >> sh) == (col >> sh)
        d = jnp.where(same, a, 0.0)
        e = jnp.where(same, 0.0, a)
    else:
        d = a
    m = eye - d
    p = d
    for _ in range(int(math.log2(bs)) - 1):
        p = _mm_hi(p, p)
        m = m + _mm_hi(m, p)
    if c == bs:
        return m
    assert c == 4 * bs
    n = _mm_hi(m, e)
    x = m + _mm_hi(_mm_hi(n, n), m)
    return x - _mm_hi(n, x)


def _gdn_kernel(t_real, t_rows, n_full, hb,
                q_ref, k_ref, v_ref, z_ref, sm_ref, cwq_ref, cwk_ref, cwv_ref, cs_ref, alog_ref, dtb_ref, ng_ref,
                s0_ref, ao_ref, nc_ref, sout_ref, xpad, g_s, b_s, s_s):
    grp = pl.program_id(1)
    width = hb * A_DV
    for part, ref in enumerate((q_ref, k_ref, v_ref)):
        xpad[part, 0:8, :] = jnp.zeros((8, width), F32)
        xpad[part, 5:8, :] = cs_ref[:, part, :]
        xpad[part, 8:8 + t_rows, :] = ref[...]
        nc_ref[:, part, :] = xpad[part, 8 + t_real - 3:8 + t_real, :]
    sm = sm_ref[...]
    valid = _iota2((t_rows, 1), 0) < t_real
    b_s[...] = jnp.where(valid, _sigmoid(sm), 0.0)
    g_s[...] = jnp.where(valid, -jnp.exp(alog_ref[...]) * _softplus(sm + dtb_ref[...]), 0.0)
    s_s[...] = s0_ref[...]
    lane = _iota2((1, SM_WIDTH), 1)

    def chunk(r0, c):
        r = _iota2((c, c), 0)
        col = _iota2((c, c), 1)
        gall = _mm_hi((r >= col).astype(F32), g_s[pl.ds(r0, c), :])
        sel = (_iota2((8, SM_WIDTH), 1) == SM_A + grp * hb + _iota2((8, SM_WIDTH), 0)).astype(F32)
        grow = lax.dot_general(sel, gall, (((1,), (1,)), ((), ())), preferred_element_type=F32,
                               precision=HIGHEST)
        bch = b_s[pl.ds(r0, c), :]
        for i in range(hb):
            ls = slice(i * A_DV, (i + 1) * A_DV)
            head = grp * hb + i

            def conv(part, cw_ref):
                xw = xpad[part, pl.ds(r0, c + 8), ls]
                acc = cw_ref[0:1, ls] * xw[5:5 + c]
                for j in range(1, CONV_WIDTH):
                    acc = acc + cw_ref[j:j + 1, ls] * xw[5 + j:5 + j + c]
                return _silu(acc)

            q = conv(0, cwq_ref)
            k = conv(1, cwk_ref)
            v = conv(2, cwv_ref)
            q = q * lax.rsqrt(jnp.sum(q * q, axis=-1, keepdims=True) + EPS) * (A_DK ** -0.5)
            k = k * lax.rsqrt(jnp.sum(k * k, axis=-1, keepdims=True) + EPS)
            gc = jnp.sum(jnp.where(lane == SM_A + head, gall, 0.0), axis=-1, keepdims=True)
            bt = jnp.sum(jnp.where(lane == SM_B + head, bch, 0.0), axis=-1, keepdims=True)
            gr = grow[i:i + 1, :]
            decay = jnp.exp(jnp.where(r >= col, gc - gr, -jnp.inf))
            a = jnp.where(r > col, bt * _mm_nt(k, k) * decay, 0.0)
            tinv = _inv_unit_lower(a, c)
            eg = jnp.exp(gc)
            sol = _mm_hi(tinv, jnp.concatenate([bt * v, (bt * eg) * k], axis=1))
            s = s_s[i]
            u = sol[:, :A_DV] - _mm(sol[:, A_DV:], s)
            o = _mm(q * eg, s) + _mm(_mm_nt(q, k) * decay, u)
            gl = gc[c - 1:c, :]
            s_s[i] = jnp.exp(gl) * s + _mm_tn(k * jnp.exp(gl - gc), u)
            on = o * lax.rsqrt(jnp.mean(o * o, axis=-1, keepdims=True) + EPS) * ng_ref[...]
            ao_ref[pl.ds(r0, c), ls] = (on * _silu(z_ref[pl.ds(r0, c), ls])).astype(ao_ref.dtype)

    lead = t_rows - n_full * CHUNK
    chunk(0, lead)
    if n_full:
        def body(ci, carry):
            chunk(pl.multiple_of(lead + ci * CHUNK, SUB), CHUNK)
            return carry
        lax.fori_loop(0, n_full, body, 0)
    sout_ref[...] = s_s[...]


def gdn_heads(p3, sm3, conv_w, conv_state, alog, dtb, norm_g, s0, t_real, n_full, hb=2):
    b, t, _ = p3.shape
    w = hb * A_DV
    ng = A_HEADS // hb
    blk = lambda off: pl.BlockSpec((None, t, w), lambda bi, g, off=off: (bi, 0, off // w + g))
    cwb = lambda part: pl.BlockSpec((CONV_WIDTH, w), lambda bi, g, part=part: (0, part * ng + g))
    row = pl.BlockSpec((1, SM_WIDTH), lambda bi, g: (0, 0))
    cs4 = conv_state.reshape(b, CONV_WIDTH - 1, 3, A_WIDTH)
    kern = functools.partial(_gdn_kernel, t_real, t, n_full, hb)
    ao, nc, sout = pl.pallas_call(
        kern,
        grid=(b, ng),
        in_specs=[blk(P_QKV), blk(P_QKV + A_WIDTH), blk(P_QKV + 2 * A_WIDTH), blk(P_Z),
                  pl.BlockSpec((None, t, SM_WIDTH), lambda bi, g: (bi, 0, 0)),
                  cwb(0), cwb(1), cwb(2),
                  pl.BlockSpec((None, CONV_WIDTH - 1, 3, w), lambda bi, g: (bi, 0, 0, g)),
                  row, row, pl.BlockSpec((1, A_DV), lambda bi, g: (0, 0)),
                  pl.BlockSpec((None, hb, A_DK, A_DV), lambda bi, g: (bi, g, 0, 0))],
        out_specs=[pl.BlockSpec((None, t, w), lambda bi, g: (bi, 0, g)),
                   pl.BlockSpec((None, CONV_WIDTH - 1, 3, w), lambda bi, g: (bi, 0, 0, g)),
                   pl.BlockSpec((None, hb, A_DK, A_DV), lambda bi, g: (bi, g, 0, 0))],
        out_shape=[jax.ShapeDtypeStruct((b, t, A_WIDTH), BF16),
                   jax.ShapeDtypeStruct((b, CONV_WIDTH - 1, 3, A_WIDTH), F32),
                   jax.ShapeDtypeStruct((b, A_HEADS, A_DK, A_DV), F32)],
        scratch_shapes=[pltpu.VMEM((3, t + 8, w), F32), pltpu.VMEM((t, SM_WIDTH), F32),
                        pltpu.VMEM((t, SM_WIDTH), F32), pltpu.VMEM((hb, A_DK, A_DV), F32)],
        compiler_params=_params("parallel", "parallel"),
        name="gdn_heads",
    )(p3, p3, p3, p3, sm3, conv_w, conv_w, conv_w, cs4, alog, dtb, norm_g.reshape(1, A_DV), s0)
    return ao, nc.reshape(b, CONV_WIDTH - 1, 3 * A_WIDTH), sout


def _gla_kernel(t_real, t_rows, n_full,
                qk_ref, v_ref, r_ref, sm_ref, w2_ref, gb_ref, ng_ref, s0_ref, co_ref, sout_ref, st_s):
    st_s[...] = s0_ref[...]

    def chunk(r0, c):
        r = _iota2((c, c), 0)
        col = _iota2((c, c), 1)
        valid = (_iota2((c, 1), 0) + r0) < t_real
        x = _mm_hi(sm_ref[pl.ds(r0, c), :], w2_ref[...]) + gb_ref[...]
        la = jnp.where(valid, (jnp.minimum(x, 0.0) - jnp.log1p(jnp.exp(-jnp.abs(x)))) / C_GATE_TAU, 0.0)
        bc = _mm_hi((r >= col).astype(F32), la)
        qk = qk_ref[pl.ds(r0, c), :]
        q = qk[:, :C_DK] * (C_DK ** -0.5)
        k = jnp.where(valid, qk[:, C_DK:], 0.0)
        v = jnp.where(valid, v_ref[pl.ds(r0, c), :], 0.0)
        st = st_s[...]
        o = _mm_nt(q * jnp.exp(bc), st)
        sb = min(c, SUB)
        rs = _iota2((sb, sb), 0)
        ls = _iota2((sb, sb), 1)
        parts = []
        for a in range(c // sb):
            lo = a * sb
            qa, ka, ba, va = q[lo:lo + sb], k[lo:lo + sb], bc[lo:lo + sb], v[lo:lo + sb]
            att = jnp.zeros((sb, sb), F32)
            for j in range(sb):
                e = jnp.exp(jnp.minimum(ba - ba[j:j + 1], 0.0))
                att = jnp.where(ls == j, jnp.sum(qa * ka[j:j + 1] * e, axis=-1, keepdims=True), att)
            oa = _mm(jnp.where(rs >= ls, att, 0.0), va)
            if a:
                edge = bc[lo - 1:lo]
                att_lo = _mm_nt(qa * jnp.exp(ba - edge), k[:lo] * jnp.exp(edge - bc[:lo]))
                oa = oa + _mm(att_lo, v[:lo])
            parts.append(oa)
        o = o + (jnp.concatenate(parts, axis=0) if len(parts) > 1 else parts[0])
        bl = bc[c - 1:c]
        st_s[...] = st * jnp.exp(bl) + _mm_tn(v, k * jnp.exp(bl - bc))
        on = o * lax.rsqrt(jnp.mean(o * o, axis=-1, keepdims=True) + EPS) * ng_ref[...]
        co_ref[pl.ds(r0, c), :] = (on * _silu(r_ref[pl.ds(r0, c), :])).astype(co_ref.dtype)

    lead = t_rows - n_full * CHUNK
    chunk(0, lead)
    if n_full:
        def body(ci, carry):
            chunk(pl.multiple_of(lead + ci * CHUNK, SUB), CHUNK)
            return carry
        lax.fori_loop(0, n_full, body, 0)
    sout_ref[...] = st_s[...]


def gla_heads(p3, sm3, w2pad, gate_b, norm_g, s0t, t_real, n_full):
    b, t, _ = p3.shape
    blk = lambda off: pl.BlockSpec((None, t, C_DV), lambda bi, h, off=off: (bi, 0, off // C_DV + h))
    kern = functools.partial(_gla_kernel, t_real, t, n_full)
    return pl.pallas_call(
        kern,
        grid=(b, C_HEADS),
        in_specs=[blk(P_CQK), blk(P_CV), blk(P_CR),
                  pl.BlockSpec((None, t, SM_WIDTH), lambda bi, h: (bi, 0, 0)),
                  pl.BlockSpec((None, SM_WIDTH, C_DK), lambda bi, h: (h, 0, 0)),
                  pl.BlockSpec((None, 1, C_DK), lambda bi, h: (h, 0, 0)),
                  pl.BlockSpec((1, C_DV), lambda bi, h: (0, 0)),
                  pl.BlockSpec((None, None, C_DV, C_DK), lambda bi, h: (bi, h, 0, 0))],
        out_specs=[pl.BlockSpec((None, t, C_DV), lambda bi, h: (bi, 0, h)),
                   pl.BlockSpec((None, None, C_DV, C_DK), lambda bi, h: (bi, h, 0, 0))],
        out_shape=[jax.ShapeDtypeStruct((b, t, C_WIDTH), BF16),
                   jax.ShapeDtypeStruct((b, C_HEADS, C_DV, C_DK), F32)],
        scratch_shapes=[pltpu.VMEM((C_DV, C_DK), F32)],
        compiler_params=_params("parallel", "parallel"),
        name="gla_heads",
    )(p3, p3, p3, sm3, w2pad, gate_b, norm_g.reshape(1, C_DV), s0t)


def _attn_kernel(tq, out_scale, lam_ref, q0_ref, q1_ref, k0_ref, k1_ref, v_ref, ng_ref, o_ref,
                 m_s, l_s, acc_s):
    head = pl.program_id(1)
    qi = pl.program_id(2)
    slope = jnp.exp2(jnp.full((1, 1), -8.0 / B_HEADS, F32) * (head + 1).astype(F32))
    scale = B_QK_DIM ** -0.5
    qs = (q0_ref[...] * scale, q1_ref[...] * scale)
    m_s[...] = jnp.full(m_s.shape, -jnp.inf, F32)
    l_s[...] = jnp.zeros(l_s.shape, F32)
    acc_s[...] = jnp.zeros(acc_s.shape, F32)
    qpos = qi * tq + _iota2((tq, tq), 0)

    def body(j, carry):
        k0 = pl.multiple_of(j * tq, 8)
        dist = qpos - (k0 + _iota2((tq, tq), 1))
        bias = slope * dist.astype(F32)
        vb = v_ref[pl.ds(k0, tq), :]
        for mi, k_ref in enumerate((k0_ref, k1_ref)):
            s = _mm_nt(qs[mi], k_ref[pl.ds(k0, tq), :]) - bias
            s = jnp.where(dist >= 0, s, -jnp.inf)
            m_prev = m_s[mi]
            m_new = jnp.maximum(m_prev, jnp.max(s, axis=-1, keepdims=True))
            alpha = jnp.exp(m_prev - m_new)
            p = jnp.exp(s - m_new)
            l_s[mi] = alpha * l_s[mi] + jnp.sum(p, axis=-1, keepdims=True)
            acc_s[mi] = alpha * acc_s[mi] + _mm(p, vb)
            m_s[mi] = m_new
        return carry

    lax.fori_loop(0, qi + 1, body, 0)
    o = acc_s[0] / l_s[0] - lam_ref[0] * (acc_s[1] / l_s[1])
    on = o * lax.rsqrt(jnp.mean(o * o, axis=-1, keepdims=True) + EPS) * ng_ref[...]
    o_ref[...] = (on * out_scale).astype(o_ref.dtype)


def diff_attention_prompt(p3, lam, norm_g, out_scale, tq):
    b, t, _ = p3.shape
    qblk = lambda off: pl.BlockSpec((None, tq, B_QK_DIM), lambda bi, h, i, off=off: (bi, i, off // B_QK_DIM + h))
    kblk = lambda off: pl.BlockSpec((None, t, B_QK_DIM), lambda bi, h, i, off=off: (bi, 0, off // B_QK_DIM + h))
    kern = functools.partial(_attn_kernel, tq, out_scale)
    return pl.pallas_call(
        kern,
        grid=(b, B_HEADS, t // tq),
        in_specs=[pl.BlockSpec(memory_space=pltpu.SMEM),
                  qblk(P_BQ), qblk(P_BQ + B_HEADS * B_QK_DIM), kblk(P_BK), kblk(P_BK + B_HEADS * B_QK_DIM),
                  pl.BlockSpec((None, t, B_V_DIM), lambda bi, h, i: (bi, 0, P_BV // B_V_DIM + h)),
                  pl.BlockSpec((1, B_V_DIM), lambda bi, h, i: (0, 0))],
        out_specs=pl.BlockSpec((None, tq, B_V_DIM), lambda bi, h, i: (bi, i, h)),
        out_shape=jax.ShapeDtypeStruct((b, t, B_WIDTH), BF16),
        scratch_shapes=[pltpu.VMEM((2, tq, 1), F32), pltpu.VMEM((2, tq, 1), F32), pltpu.VMEM((2, tq, B_V_DIM), F32)],
        compiler_params=_params("parallel", "parallel", "arbitrary"),
        name="diff_attn_prompt",
    )(lam, p3, p3, p3, p3, p3, norm_g.reshape(1, B_V_DIM))


def _attn_decode_kernel(npp, n_steps, past_len, out_scale, tbl_ref, lam_ref, q_ref, kn_ref, vn_ref, ng_ref, *rest):
    k_refs = rest[:npp]
    v_refs = rest[npp:2 * npp]
    o_ref = rest[2 * npp]
    m_s, l_s, acc_s = rest[2 * npp + 1:]
    step = pl.program_id(1)
    nhm = 2 * B_HEADS
    scale = B_QK_DIM ** -0.5
    row = _iota2((nhm, 1), 0)
    slope = jnp.exp2(-8.0 * ((row & (B_HEADS - 1)) + 1).astype(F32) / B_HEADS)
    q = q_ref[...] * scale

    @pl.when(step == 0)
    def _():
        m_s[...] = jnp.full(m_s.shape, -jnp.inf, F32)
        l_s[...] = jnp.zeros(l_s.shape, F32)
        acc_s[...] = jnp.zeros(acc_s.shape, F32)

    rows8 = _iota2((nhm, PAGE_SIZE), 0)
    rows8v = _iota2((nhm, B_V_DIM), 0)
    for pi in range(npp):
        k_ref, v_ref = k_refs[pi], v_refs[pi]
        s = jnp.zeros((nhm, PAGE_SIZE), F32)
        for hm in range(nhm):
            s = jnp.where(rows8 == hm, _mm_nt(q, k_ref[:, hm, :]), s)
        kpos = (step * npp + pi) * PAGE_SIZE + _iota2((1, PAGE_SIZE), 1)
        s = s - slope * (past_len - kpos).astype(F32)
        m_prev = m_s[...]
        m_new = jnp.maximum(m_prev, jnp.max(s, axis=-1, keepdims=True))
        alpha = jnp.exp(m_prev - m_new)
        p = jnp.exp(s - m_new)
        l_s[...] = alpha * l_s[...] + jnp.sum(p, axis=-1, keepdims=True)
        pv = jnp.zeros((nhm, B_V_DIM), F32)
        for h in range(B_HEADS):
            pv = jnp.where((rows8v & (B_HEADS - 1)) == h, _mm(p, v_ref[:, h, :]), pv)
        acc_s[...] = alpha * acc_s[...] + pv
        m_s[...] = m_new

    @pl.when(step == n_steps - 1)
    def _():
        s_new = jnp.sum(q * kn_ref[...], axis=-1, keepdims=True)
        m_prev = m_s[...]
        m_new = jnp.maximum(m_prev, s_new)
        alpha = jnp.exp(m_prev - m_new)
        p_new = jnp.exp(s_new - m_new)
        l_fin = alpha * l_s[...] + p_new
        vn = vn_ref[...]
        acc = alpha * acc_s[...] + p_new * jnp.concatenate([vn, vn], axis=0)
        on = acc / l_fin
        o = on[:B_HEADS] - lam_ref[0] * on[B_HEADS:]
        o = o * lax.rsqrt(jnp.mean(o * o, axis=-1, keepdims=True) + EPS) * ng_ref[...]
        o_ref[...] = (o * out_scale).astype(o_ref.dtype)


def diff_attention_decode(layer, q, k_new, v_new, cache_k, cache_v, page_table, lam, norm_g, out_scale, npp=4):
    b = q.shape[0]
    n_pages = page_table.shape[1]
    n_steps = n_pages // npp
    nhm = 2 * B_HEADS
    kspec = lambda pi: pl.BlockSpec((None, None, PAGE_SIZE, nhm, B_QK_DIM),
                                    lambda bi, s, tbl, pi=pi: (layer, tbl[bi, s * npp + pi], 0, 0, 0))
    vspec = lambda pi: pl.BlockSpec((None, None, PAGE_SIZE, B_HEADS, B_V_DIM),
                                    lambda bi, s, tbl, pi=pi: (layer, tbl[bi, s * npp + pi], 0, 0, 0))
    kern = functools.partial(_attn_decode_kernel, npp, n_steps, n_pages * PAGE_SIZE, out_scale)
    grid_spec = pltpu.PrefetchScalarGridSpec(
        num_scalar_prefetch=1,
        grid=(b, n_steps),
        in_specs=[pl.BlockSpec(memory_space=pltpu.SMEM),
                  pl.BlockSpec((None, nhm, B_QK_DIM), lambda bi, s, tbl: (bi, 0, 0)),
                  pl.BlockSpec((None, nhm, B_QK_DIM), lambda bi, s, tbl: (bi, 0, 0)),
                  pl.BlockSpec((None, B_HEADS, B_V_DIM), lambda bi, s, tbl: (bi, 0, 0)),
                  pl.BlockSpec((1, B_V_DIM), lambda bi, s, tbl: (0, 0))]
                 + [kspec(pi) for pi in range(npp)] + [vspec(pi) for pi in range(npp)],
        out_specs=pl.BlockSpec((None, B_HEADS, B_V_DIM), lambda bi, s, tbl: (bi, 0, 0)),
        scratch_shapes=[pltpu.VMEM((nhm, 1), F32), pltpu.VMEM((nhm, 1), F32), pltpu.VMEM((nhm, B_V_DIM), F32)],
    )
    return pl.pallas_call(
        kern,
        grid_spec=grid_spec,
        out_shape=jax.ShapeDtypeStruct((b, B_HEADS, B_V_DIM), F32),
        compiler_params=_params("parallel", "arbitrary"),
        name="diff_attn_decode",
    )(page_table, lam, q, k_new, v_new, norm_g.reshape(1, B_V_DIM), *([cache_k] * npp), *([cache_v] * npp))


def _split_in_proj(w):
    idx = [0]
    for s in IN_SIZES:
        idx.append(idx[-1] + s)
    return [w[:, idx[i]:idx[i + 1]] for i in range(len(IN_SIZES))]


def _prep_in_proj(w):
    a_qkv, a_z, a_b, a_a, b_q, b_k, b_v, c_q, c_k, c_v, c_r, c_g = _split_in_proj(w)
    d = w.shape[0]
    c_qk = jnp.concatenate([c_q.reshape(d, C_HEADS, C_DK), c_k.reshape(d, C_HEADS, C_DK)], axis=2)
    main = jnp.concatenate([a_qkv, a_z, b_q, b_k, b_v, c_qk.reshape(d, 2 * C_HEADS * C_DK), c_v, c_r], axis=1)
    pad = jnp.zeros((d, SM_WIDTH - 2 * A_HEADS - C_GATE_RANK), w.dtype)
    small = jnp.concatenate([a_b, a_a, c_g, pad], axis=1)
    return main.astype(BF16), small.astype(BF16)


def _lane_row(vals, offset):
    return jnp.zeros((1, SM_WIDTH), F32).at[0, offset:offset + vals.shape[0]].set(vals.astype(F32))


def _layer(layer, h, b, t, t_real, n_full, tm, tm_ff, states, attn_fn, prm):
    (norm1_g, w_main, w_small, a_conv_w, a_log, a_dt_bias, a_norm_g, lam, b_norm_g, w2pad, gate_b, c_norm_g,
     wo_a, wo_b, wo_c, norm2_g, w_gate, w_up, w_down) = prm
    conv_state, s_a, s_c_t = states
    u = rmsnorm(h, norm1_g, BF16, tm)
    p = matmul(u, w_main, tm, 1152)
    sm = matmul(u, w_small, tm, SM_WIDTH)
    p3 = p.reshape(b, t, P_WIDTH)
    sm3 = sm.reshape(b, t, SM_WIDTH)
    ao, new_conv, s_a_new = gdn_heads(p3, sm3, a_conv_w, conv_state, _lane_row(a_log, SM_A),
                                      _lane_row(a_dt_bias, SM_A), a_norm_g, s_a, t_real, n_full)
    out_scale = 1.0 - (0.8 - 0.6 * math.exp(-0.3 * layer))
    bo, bk, bv = attn_fn(p3, lam, b_norm_g, out_scale)
    co, s_c_t_new = gla_heads(p3, sm3, w2pad, gate_b, c_norm_g, s_c_t, t_real, n_full)
    m = b * t
    h = out_proj(ao.reshape(m, A_WIDTH), bo.reshape(m, B_WIDTH), co.reshape(m, C_WIDTH), wo_a, wo_b, wo_c, h, tm, 1024)
    u2 = rmsnorm(h, norm2_g, BF16, tm)
    act = ffn_gateup(u2, w_gate, w_up, tm_ff, 256)
    h = ffn_down(act, w_down, h, tm, 256)
    return h, new_conv, s_a_new, s_c_t_new, bk, bv


def kernel(x_prompt, x_sample, cache_k, cache_v, page_table, state_a_conv, state_a_rec, state_c_rec, meta_tokens, norm1_g, w_in, a_conv_w, a_log, a_dt_bias, a_norm_g, b_lambda_q1, b_lambda_k1, b_lambda_q2, b_lambda_k2, b_norm_g, c_gate_w2, c_gate_b, c_norm_g, w_o, norm2_g, w_gate, w_up, w_down, final_norm_g):
    bp, seq, d = x_prompt.shape
    bs = x_sample.shape[0]
    tp = seq + N_META
    ts = 8
    hp = jnp.concatenate([jnp.broadcast_to(meta_tokens[None], (bp, N_META, d)), x_prompt], axis=1).reshape(bp * tp, d)
    hs = jnp.pad(x_sample, ((0, 0), (0, ts - 1), (0, 0))).reshape(bs * ts, d)
    conv0 = jnp.zeros((bp, CONV_WIDTH - 1, 3 * A_WIDTH), F32)
    sa0 = jnp.zeros((bp, A_HEADS, A_DK, A_DV), F32)
    sc0 = jnp.zeros((bp, C_HEADS, C_DV, C_DK), F32)
    tm_p = _tile(tp * bp, 688, 16)
    tm_ff = _tile(tp * bp, 1376, 16)
    tm_s = bs * ts
    outs = [[] for _ in range(10)]
    for l in range(DEPTH):
        w_main, w_small = _prep_in_proj(w_in[l])
        lam_init = 0.8 - 0.6 * math.exp(-0.3 * l)
        lam = (jnp.exp(jnp.sum(b_lambda_q1[l] * b_lambda_k1[l])) - jnp.exp(jnp.sum(b_lambda_q2[l] * b_lambda_k2[l]))
               + lam_init).reshape(1).astype(F32)
        w2pad = jnp.zeros((C_HEADS, SM_WIDTH, C_DK), F32).at[:, SM_G:SM_G + C_GATE_RANK, :].set(
            jnp.transpose(c_gate_w2[l].reshape(C_GATE_RANK, C_HEADS, C_DK), (1, 0, 2)))
        wo = w_o[l].astype(BF16)
        prm = (norm1_g[l], w_main, w_small, a_conv_w[l], a_log[l], a_dt_bias[l], a_norm_g[l], lam, b_norm_g[l],
               w2pad, c_gate_b[l].reshape(C_HEADS, 1, C_DK), c_norm_g[l],
               wo[:A_WIDTH], wo[A_WIDTH:A_WIDTH + B_WIDTH], wo[A_WIDTH + B_WIDTH:], norm2_g[l],
               w_gate[l].astype(BF16), w_up[l].astype(BF16), w_down[l].astype(BF16))

        def attn_prompt(p3, lam, g, out_scale):
            bo = diff_attention_prompt(p3, lam, g, out_scale, tq=_tile(tp, 344, 8))
            return bo, p3[:, :, P_BK:P_BK + 2 * B_HEADS * B_QK_DIM], p3[:, :, P_BV:P_BV + B_WIDTH]

        def attn_sample(p3, lam, g, out_scale, l=l):
            row = p3[:, 0]
            q = row[:, P_BQ:P_BQ + 2 * B_HEADS * B_QK_DIM].reshape(bs, 2 * B_HEADS, B_QK_DIM)
            k_new = row[:, P_BK:P_BK + 2 * B_HEADS * B_QK_DIM].reshape(bs, 2 * B_HEADS, B_QK_DIM)
            v_new = row[:, P_BV:P_BV + B_WIDTH].reshape(bs, B_HEADS, B_V_DIM)
            bo = diff_attention_decode(l, q, k_new, v_new, cache_k, cache_v, page_table, lam, g, out_scale)
            bo = jnp.pad(bo.astype(BF16).reshape(bs, 1, B_WIDTH), ((0, 0), (0, ts - 1), (0, 0)))
            return bo, k_new, v_new

        hp, c1, a1, s1, k1, v1 = _layer(l, hp, bp, tp, tp, seq // CHUNK, tm_p, tm_ff, (conv0, sa0, sc0), attn_prompt, prm)
        st_s = (state_a_conv[l], state_a_rec[l], jnp.swapaxes(state_c_rec[l], -1, -2))
        hs, c2, a2, s2, k2, v2 = _layer(l, hs, bs, ts, 1, 0, tm_s, tm_s, st_s, attn_sample, prm)
        for lst, val in zip(outs, (k1.reshape(bp, tp, 2 * B_HEADS, B_QK_DIM), v1.reshape(bp, tp, B_HEADS, B_V_DIM),
                                   k2.reshape(bs, 1, 2 * B_HEADS, B_QK_DIM), v2.reshape(bs, 1, B_HEADS, B_V_DIM),
                                   c1, c2, a1, a2, jnp.swapaxes(s1, -1, -2), jnp.swapaxes(s2, -1, -2))):
            lst.append(val)
    y_prompt = rmsnorm(hp, final_norm_g, F32, tm_p).reshape(bp, tp, d)[:, N_META:]
    y_sample = rmsnorm(hs, final_norm_g, F32, tm_s).reshape(bs, ts, d)[:, :1]
    return (y_prompt, y_sample) + tuple(jnp.stack(o) for o in outs)
```

```python
import functools
import math

import jax
import jax.numpy as jnp
from jax import lax
from jax.experimental import pallas as pl
from jax.experimental.pallas import tpu as pltpu

F32 = jnp.float32
BF16 = jnp.bfloat16
HIGHEST = lax.Precision.HIGHEST

D_MODEL = 4096
N_META = 16
DEPTH = 4
PAGE_SIZE = 128
CHUNK = 64
SUB = 16
GLA_SUB = 16
A_HEADS, A_DK, A_DV, A_WIDTH, CONV_WIDTH = 12, 128, 128, 1536, 4
B_HEADS, B_QK_DIM, B_V_DIM, B_WIDTH = 4, 128, 256, 1024
C_HEADS, C_DK, C_DV, C_WIDTH, C_GATE_RANK, C_GATE_TAU = 4, 192, 384, 1536, 16, 16.0
D_FF = 11008
EPS = 1e-6
IN_SIZES = (3 * A_WIDTH, A_WIDTH, A_HEADS, A_HEADS, 2 * B_HEADS * B_QK_DIM, 2 * B_HEADS * B_QK_DIM, B_WIDTH,
            C_HEADS * C_DK, C_HEADS * C_DK, C_WIDTH, C_WIDTH, C_GATE_RANK)

P_QKV, P_Z, P_BQ, P_BK, P_BV, P_CQK, P_CV, P_CR, P_WIDTH = 0, 4608, 6144, 7168, 8192, 9216, 10752, 12288, 13824
SM_B, SM_A, SM_G, SM_WIDTH = 0, A_HEADS, 2 * A_HEADS, 128

VMEM_LIMIT_BYTES = 56 * 1024 * 1024


def _tile(n, cap, align):
    return max(t for t in range(align, min(n, cap) + 1, align) if n % t == 0)


def _params(*sem):
    return pltpu.CompilerParams(dimension_semantics=sem, vmem_limit_bytes=VMEM_LIMIT_BYTES)


def _sigmoid(x):
    return 1.0 / (1.0 + jnp.exp(-x))


def _silu(x):
    return x * _sigmoid(x)


def _softplus(x):
    return jnp.maximum(x, 0.0) + jnp.log1p(jnp.exp(-jnp.abs(x)))


def _mm(a, b):
    return jnp.dot(a.astype(BF16), b.astype(BF16), preferred_element_type=F32)


def _mm_nt(a, b):
    return lax.dot_general(a.astype(BF16), b.astype(BF16), (((1,), (1,)), ((), ())), preferred_element_type=F32)


def _mm_tn(a, b):
    return lax.dot_general(a.astype(BF16), b.astype(BF16), (((0,), (0,)), ((), ())), preferred_element_type=F32)


def _mm_hi(a, b):
    return jnp.dot(a, b, preferred_element_type=F32, precision=HIGHEST)


def _iota2(shape, dim):
    return lax.broadcasted_iota(jnp.int32, shape, dim)


def _bdot(a, b, ca, cb):
    return lax.dot_general(a.astype(BF16), b.astype(BF16), (((ca,), (cb,)), ((0,), (0,))), preferred_element_type=F32)


def _bmm(a, b):
    return _bdot(a, b, 2, 1)


def _bmm_nt(a, b):
    return _bdot(a, b, 2, 2)


def _bmm_tn(a, b):
    return _bdot(a, b, 1, 1)


def _split2(x):
    hi = x.astype(BF16).astype(F32)
    return hi, x - hi


def _mm3(a, b):
    ah, al = _split2(a)
    bh, bl = _split2(b)
    return _bmm(jnp.concatenate([ah, ah, al], axis=2), jnp.concatenate([bh, bl, bh], axis=1))


def _split3(x):
    p1 = x.astype(BF16).astype(F32)
    r1 = x - p1
    p2 = r1.astype(BF16).astype(F32)
    return p1, p2, r1 - p2


def _cumsum_rows(x, c):
    tri = ((_iota2((c, 3 * c), 1) & (c - 1)) <= _iota2((c, 3 * c), 0)).astype(BF16)
    return jnp.dot(tri, jnp.concatenate(_split3(x), axis=0).astype(BF16), preferred_element_type=F32)


def _rmsnorm_kernel(x_ref, g_ref, o_ref):
    x = x_ref[...]
    y = x * lax.rsqrt(jnp.mean(x * x, axis=-1, keepdims=True) + EPS)
    o_ref[...] = (y * g_ref[...]).astype(o_ref.dtype)


def rmsnorm(x, g, out_dtype, tm):
    m, d = x.shape
    return pl.pallas_call(
        _rmsnorm_kernel,
        grid=(m // tm,),
        in_specs=[pl.BlockSpec((tm, d), lambda i: (i, 0)), pl.BlockSpec((1, d), lambda i: (0, 0))],
        out_specs=pl.BlockSpec((tm, d), lambda i: (i, 0)),
        out_shape=jax.ShapeDtypeStruct((m, d), out_dtype),
        compiler_params=_params("parallel"),
        name="rmsnorm",
    )(x, g.reshape(1, d))


def _mm_kernel(x_ref, w_ref, o_ref):
    o_ref[...] = jnp.dot(x_ref[...], w_ref[...], preferred_element_type=F32).astype(o_ref.dtype)


def matmul(x, w, tm, tn, out_dtype=F32):
    m, k = x.shape
    n = w.shape[1]
    return pl.pallas_call(
        _mm_kernel,
        grid=(n // tn, m // tm),
        in_specs=[pl.BlockSpec((tm, k), lambda j, i: (i, 0)), pl.BlockSpec((k, tn), lambda j, i: (0, j))],
        out_specs=pl.BlockSpec((tm, tn), lambda j, i: (i, j)),
        out_shape=jax.ShapeDtypeStruct((m, n), out_dtype),
        compiler_params=_params("parallel", "parallel"),
        name="proj_in",
    )(x, w)


def _mm3_res_kernel(xa_ref, xb_ref, xc_ref, wa_ref, wb_ref, wc_ref, r_ref, o_ref):
    acc = jnp.dot(xa_ref[...], wa_ref[...], preferred_element_type=F32)
    acc += jnp.dot(xb_ref[...], wb_ref[...], preferred_element_type=F32)
    acc += jnp.dot(xc_ref[...], wc_ref[...], preferred_element_type=F32)
    o_ref[...] = r_ref[...] + acc


def out_proj(xa, xb, xc, wa, wb, wc, res, tm, tn):
    m = xa.shape[0]
    n = wa.shape[1]
    xs = lambda a: pl.BlockSpec((tm, a.shape[1]), lambda j, i: (i, 0))
    ws = lambda a: pl.BlockSpec((a.shape[0], tn), lambda j, i: (0, j))
    return pl.pallas_call(
        _mm3_res_kernel,
        grid=(n // tn, m // tm),
        in_specs=[xs(xa), xs(xb), xs(xc), ws(wa), ws(wb), ws(wc), pl.BlockSpec((tm, tn), lambda j, i: (i, j))],
        out_specs=pl.BlockSpec((tm, tn), lambda j, i: (i, j)),
        out_shape=jax.ShapeDtypeStruct((m, n), F32),
        compiler_params=_params("parallel", "parallel"),
        name="proj_out",
    )(xa, xb, xc, wa, wb, wc, res)


def _gateup_kernel(x_ref, wg_ref, wu_ref, o_ref):
    x = x_ref[...]
    g = jnp.dot(x, wg_ref[...], preferred_element_type=F32)
    u = jnp.dot(x, wu_ref[...], preferred_element_type=F32)
    o_ref[...] = (_silu(g) * u).astype(o_ref.dtype)


def ffn_gateup(x, wg, wu, tm, tn):
    m, k = x.shape
    n = wg.shape[1]
    return pl.pallas_call(
        _gateup_kernel,
        grid=(m // tm, n // tn),
        in_specs=[pl.BlockSpec((tm, k), lambda i, j: (i, 0)), pl.BlockSpec((k, tn), lambda i, j: (0, j)),
                  pl.BlockSpec((k, tn), lambda i, j: (0, j))],
        out_specs=pl.BlockSpec((tm, tn), lambda i, j: (i, j)),
        out_shape=jax.ShapeDtypeStruct((m, n), BF16),
        compiler_params=_params("parallel", "parallel"),
        name="ffn_gateup",
    )(x, wg, wu)


def _down_kernel(x_ref, w_ref, r_ref, o_ref):
    o_ref[...] = r_ref[...] + jnp.dot(x_ref[...], w_ref[...], preferred_element_type=F32)


def ffn_down(x, w, res, tm, tn):
    m, k = x.shape
    n = w.shape[1]
    return pl.pallas_call(
        _down_kernel,
        grid=(m // tm, n // tn),
        in_specs=[pl.BlockSpec((tm, k), lambda i, j: (i, 0)), pl.BlockSpec((k, tn), lambda i, j: (0, j)),
                  pl.BlockSpec((tm, tn), lambda i, j: (i, j))],
        out_specs=pl.BlockSpec((tm, tn), lambda i, j: (i, j)),
        out_shape=jax.ShapeDtypeStruct((m, n), F32),
        compiler_params=_params("parallel", "parallel"),
        name="ffn_down",
    )(x, w, res)


def _solve_unit_lower(a, rhs, c):
    r = _iota2((c, c), 0)
    col = _iota2((c, c), 1)
    eye = (r == col).astype(F32)
    bs = min(c, SUB)
    if c > bs:
        sh = int(math.log2(bs))
        same = (r >> sh) == (col >> sh)
        d = jnp.where(same, a, 0.0)
        e = jnp.where(same, 0.0, a)
    else:
        d = a
    m = eye - d
    p = d
    for _ in range(int(math.log2(bs)) - 1):
        p = _mm3(p, p)
        m = m + _mm3(m, p)
    y = _mm3(m, rhs)
    if c == bs:
        return y
    assert c == 4 * bs
    n = _mm3(m, e)
    y = y + _mm3(_mm3(n, n), y)
    return y - _mm3(n, y)


def _gdn_kernel(t_real, t_rows, n_full, hb, nc_main,
                q_ref, k_ref, v_ref, z_ref, sm_ref, cwq_ref, cwk_ref, cwv_ref, cs_ref, alog_ref, dtb_ref, ng_ref,
                s0_ref, ao_ref, nc_ref, sout_ref, xpad, g_s, b_s, s_s):
    grp = pl.program_id(1)
    width = hb * A_DV
    for part, ref in enumerate((q_ref, k_ref, v_ref)):
        xpad[part, 0:8, :] = jnp.zeros((8, width), F32)
        xpad[part, 5:8, :] = cs_ref[:, part, :]
        xpad[part, 8:8 + t_rows, :] = ref[...]
        nc_ref[:, part, :] = xpad[part, 8 + t_real - 3:8 + t_real, :]
    sm = sm_ref[...]
    valid = _iota2((t_rows, 1), 0) < t_real
    b_s[...] = jnp.where(valid, _sigmoid(sm), 0.0)
    g_s[...] = jnp.where(valid, -jnp.exp(alog_ref[...]) * _softplus(sm + dtb_ref[...]), 0.0)
    s_s[...] = s0_ref[...]
    lane = _iota2((1, SM_WIDTH), 1)

    sel = (_iota2((8, SM_WIDTH), 1) == SM_A + grp * hb + _iota2((8, SM_WIDTH), 0)).astype(F32)

    def group(r0, c, nc):
        rows = nc * c
        r = _iota2((c, c), 0)
        col = _iota2((c, c), 1)
        gcs, grs, bts = [], [], []
        for g in range(nc):
            rr = pl.ds(r0 + g * c, c)
            gall = _cumsum_rows(g_s[rr, :], c)
            grow = sum(_mm_nt(sel, piece) for piece in _split3(gall))
            bch = b_s[rr, :]
            for i in range(hb):
                head = grp * hb + i
                gcs.append(jnp.sum(jnp.where(lane == SM_A + head, gall, 0.0), axis=-1, keepdims=True))
                bts.append(jnp.sum(jnp.where(lane == SM_B + head, bch, 0.0), axis=-1, keepdims=True))
                grs.append(grow[i:i + 1, :])
        gc, bt, gr = jnp.stack(gcs), jnp.stack(bts), jnp.stack(grs)

        def conv(part, cw_ref):
            xw = xpad[part, pl.ds(r0, rows + 8), :]
            acc = cw_ref[0:1, :] * xw[5:5 + rows]
            for j in range(1, CONV_WIDTH):
                acc = acc + cw_ref[j:j + 1, :] * xw[5 + j:5 + j + rows]
            y = _silu(acc)
            return jnp.stack([y[g * c:(g + 1) * c, i * A_DV:(i + 1) * A_DV] for g in range(nc) for i in range(hb)])

        q = conv(0, cwq_ref)
        k = conv(1, cwk_ref)
        v = conv(2, cwv_ref)
        q = q * lax.rsqrt(jnp.sum(q * q, axis=-1, keepdims=True) + EPS) * (A_DK ** -0.5)
        k = k * lax.rsqrt(jnp.sum(k * k, axis=-1, keepdims=True) + EPS)
        decay = jnp.exp(jnp.where(r >= col, gc - gr, -jnp.inf))
        a = jnp.where(r > col, bt * _bmm_nt(k, k) * decay, 0.0)
        eg = jnp.exp(gc)
        sol = _solve_unit_lower(a, jnp.concatenate([bt * v, (bt * eg) * k], axis=2), c)
        qkd = _bmm_nt(q, k) * decay
        qe = q * eg
        gl = gc[:, c - 1:c, :]
        khat = k * jnp.exp(gl - gc)
        egl = jnp.exp(gl)
        for g in range(nc):
            sl = slice(g * hb, (g + 1) * hb)
            s = s_s[...]
            u = sol[sl, :, :A_DV] - _bmm(sol[sl, :, A_DV:], s)
            o = _bmm(qe[sl], s) + _bmm(qkd[sl], u)
            s_s[...] = egl[sl] * s + _bmm_tn(khat[sl], u)
            on = o * lax.rsqrt(jnp.mean(o * o, axis=-1, keepdims=True) + EPS) * ng_ref[...]
            rr = pl.ds(r0 + g * c, c)
            for i in range(hb):
                ls = slice(i * A_DV, (i + 1) * A_DV)
                ao_ref[rr, ls] = (on[i] * _silu(z_ref[rr, ls])).astype(ao_ref.dtype)

    lead = t_rows - n_full * CHUNK
    group(0, lead, 1)
    if n_full:
        assert n_full % nc_main == 0

        def body(ci, carry):
            group(pl.multiple_of(lead + ci * (nc_main * CHUNK), SUB), CHUNK, nc_main)
            return carry
        lax.fori_loop(0, n_full // nc_main, body, 0)
    sout_ref[...] = s_s[...]


def gdn_heads(p3, sm3, conv_w, conv_state, alog, dtb, norm_g, s0, t_real, n_full, hb=2, nc_main=4):
    b, t, _ = p3.shape
    w = hb * A_DV
    ng = A_HEADS // hb
    blk = lambda off: pl.BlockSpec((None, t, w), lambda bi, g, off=off: (bi, 0, off // w + g))
    cwb = lambda part: pl.BlockSpec((CONV_WIDTH, w), lambda bi, g, part=part: (0, part * ng + g))
    row = pl.BlockSpec((1, SM_WIDTH), lambda bi, g: (0, 0))
    cs4 = conv_state.reshape(b, CONV_WIDTH - 1, 3, A_WIDTH)
    kern = functools.partial(_gdn_kernel, t_real, t, n_full, hb, math.gcd(n_full, nc_main))
    ao, nc, sout = pl.pallas_call(
        kern,
        grid=(b, ng),
        in_specs=[blk(P_QKV), blk(P_QKV + A_WIDTH), blk(P_QKV + 2 * A_WIDTH), blk(P_Z),
                  pl.BlockSpec((None, t, SM_WIDTH), lambda bi, g: (bi, 0, 0)),
                  cwb(0), cwb(1), cwb(2),
                  pl.BlockSpec((None, CONV_WIDTH - 1, 3, w), lambda bi, g: (bi, 0, 0, g)),
                  row, row, pl.BlockSpec((1, A_DV), lambda bi, g: (0, 0)),
                  pl.BlockSpec((None, hb, A_DK, A_DV), lambda bi, g: (bi, g, 0, 0))],
        out_specs=[pl.BlockSpec((None, t, w), lambda bi, g: (bi, 0, g)),
                   pl.BlockSpec((None, CONV_WIDTH - 1, 3, w), lambda bi, g: (bi, 0, 0, g)),
                   pl.BlockSpec((None, hb, A_DK, A_DV), lambda bi, g: (bi, g, 0, 0))],
        out_shape=[jax.ShapeDtypeStruct((b, t, A_WIDTH), BF16),
                   jax.ShapeDtypeStruct((b, CONV_WIDTH - 1, 3, A_WIDTH), F32),
                   jax.ShapeDtypeStruct((b, A_HEADS, A_DK, A_DV), F32)],
        scratch_shapes=[pltpu.VMEM((3, t + 8, w), F32), pltpu.VMEM((t, SM_WIDTH), F32),
                        pltpu.VMEM((t, SM_WIDTH), F32), pltpu.VMEM((hb, A_DK, A_DV), F32)],
        compiler_params=_params("parallel", "parallel"),
        name="gdn_heads",
    )(p3, p3, p3, p3, sm3, conv_w, conv_w, conv_w, cs4, alog, dtb, norm_g.reshape(1, A_DV), s0)
    return ao, nc.reshape(b, CONV_WIDTH - 1, 3 * A_WIDTH), sout


def _gla_kernel(t_real, t_rows, n_full, nc_main,
                qk_ref, v_ref, r_ref, sm_ref, w2_ref, gb_ref, ng_ref, s0_ref, co_ref, sout_ref, st_s, la_s):
    st_s[...] = s0_ref[...]
    valid_all = _iota2((t_rows, 1), 0) < t_real
    x = _mm_hi(sm_ref[...], w2_ref[...]) + gb_ref[...]
    la_s[...] = jnp.where(valid_all, (jnp.minimum(x, 0.0) - jnp.log1p(jnp.exp(-jnp.abs(x)))) / C_GATE_TAU, 0.0)

    def group(r0, c, nc):
        rows = nc * c
        sb = min(c, GLA_SUB)
        nsub = c // sb
        nb = nc * nsub
        tri = ((_iota2((nc, c, 3 * c), 2) & (c - 1)) <= _iota2((nc, c, 3 * c), 1))
        la = la_s[pl.ds(r0, rows), :].reshape(nc, c, C_DK)
        bc = _bmm(tri.astype(BF16), jnp.concatenate(_split3(la), axis=1))
        valid = (_iota2((rows, 1), 0) + r0) < t_real
        qk = qk_ref[pl.ds(r0, rows), :]
        q = (qk[:, :C_DK] * (C_DK ** -0.5)).reshape(nc, c, C_DK)
        k = jnp.where(valid, qk[:, C_DK:], 0.0).reshape(nc, c, C_DK)
        v = jnp.where(valid, v_ref[pl.ds(r0, rows), :], 0.0).reshape(nc, c, C_DV)
        q4, k4, b4 = (z.reshape(nb, sb, C_DK) for z in (q, k, bc))
        ls = _iota2((nb, sb, sb), 2)
        att = jnp.zeros((nb, sb, sb), F32)
        for j in range(sb):
            top = (j // 8) * 8
            e = jnp.exp(jnp.minimum(b4[:, top:] - b4[:, j:j + 1, :], 0.0))
            colj = jnp.sum(q4[:, top:] * k4[:, j:j + 1, :] * e, axis=-1, keepdims=True)
            if top:
                colj = jnp.concatenate([jnp.zeros((nb, top, 1), F32), colj], axis=1)
            att = jnp.where(ls == j, colj, att)
        att = jnp.where(_iota2((nb, sb, sb), 1) >= ls, att, 0.0)
        od = _bmm(att, v.reshape(nb, sb, C_DV)).reshape(nc, c, C_DV)
        parts = [od[:, :sb]]
        for a in range(1, nsub):
            lo = a * sb
            edge = bc[:, lo - 1:lo, :]
            qt = q[:, lo:lo + sb] * jnp.exp(bc[:, lo:lo + sb] - edge)
            kt = k[:, :lo] * jnp.exp(edge - bc[:, :lo])
            parts.append(od[:, lo:lo + sb] + _bmm(_bmm_nt(qt, kt), v[:, :lo]))
        o_intra = jnp.concatenate(parts, axis=1) if nsub > 1 else od
        bl = bc[:, c - 1:c, :]
        kv = _bmm_tn(k * jnp.exp(bl - bc), v)
        qe = q * jnp.exp(bc)
        ebl = jnp.exp(bl)
        for g in range(nc):
            st = st_s[...]
            o = _mm(qe[g], st) + o_intra[g]
            st_s[...] = jnp.broadcast_to(ebl[g], (8, C_DK)).T[:, 0:1] * st + kv[g]
            rr = pl.ds(r0 + g * c, c)
            on = o * lax.rsqrt(jnp.mean(o * o, axis=-1, keepdims=True) + EPS) * ng_ref[...]
            co_ref[rr, :] = (on * _silu(r_ref[rr, :])).astype(co_ref.dtype)

    lead = t_rows - n_full * CHUNK
    group(0, lead, 1)
    if n_full:
        assert n_full % nc_main == 0

        def body(ci, carry):
            group(pl.multiple_of(lead + ci * (nc_main * CHUNK), SUB), CHUNK, nc_main)
            return carry
        lax.fori_loop(0, n_full // nc_main, body, 0)
    sout_ref[...] = st_s[...]


def gla_heads(p3, sm3, w2pad, gate_b, norm_g, s0, t_real, n_full, nc_main=4):
    b, t, _ = p3.shape
    blk = lambda off: pl.BlockSpec((None, t, C_DV), lambda bi, h, off=off: (bi, 0, off // C_DV + h))
    kern = functools.partial(_gla_kernel, t_real, t, n_full, math.gcd(n_full, nc_main))
    return pl.pallas_call(
        kern,
        grid=(b, C_HEADS),
        in_specs=[blk(P_CQK), blk(P_CV), blk(P_CR),
                  pl.BlockSpec((None, t, SM_WIDTH), lambda bi, h: (bi, 0, 0)),
                  pl.BlockSpec((None, SM_WIDTH, C_DK), lambda bi, h: (h, 0, 0)),
                  pl.BlockSpec((None, 1, C_DK), lambda bi, h: (h, 0, 0)),
                  pl.BlockSpec((1, C_DV), lambda bi, h: (0, 0)),
                  pl.BlockSpec((None, None, C_DK, C_DV), lambda bi, h: (bi, h, 0, 0))],
        out_specs=[pl.BlockSpec((None, t, C_DV), lambda bi, h: (bi, 0, h)),
                   pl.BlockSpec((None, None, C_DK, C_DV), lambda bi, h: (bi, h, 0, 0))],
        out_shape=[jax.ShapeDtypeStruct((b, t, C_WIDTH), BF16),
                   jax.ShapeDtypeStruct((b, C_HEADS, C_DK, C_DV), F32)],
        scratch_shapes=[pltpu.VMEM((C_DK, C_DV), F32), pltpu.VMEM((t, C_DK), F32)],
        compiler_params=_params("parallel", "parallel"),
        name="gla_heads",
    )(p3, p3, p3, sm3, w2pad, gate_b, norm_g.reshape(1, C_DV), s0)


def _attn_kernel(tq, out_scale, lam_ref, q0_ref, q1_ref, k0_ref, k1_ref, v_ref, ng_ref, o_ref,
                 m_s, l_s, acc_s):
    head = pl.program_id(1)
    qi = pl.program_id(2)
    slope = jnp.exp2(jnp.full((1, 1), -8.0 / B_HEADS, F32) * (head + 1).astype(F32))
    scale = B_QK_DIM ** -0.5
    qs = (q0_ref[...] * scale, q1_ref[...] * scale)
    m_s[...] = jnp.full(m_s.shape, -jnp.inf, F32)
    l_s[...] = jnp.zeros(l_s.shape, F32)
    acc_s[...] = jnp.zeros(acc_s.shape, F32)
    qpos = qi * tq + _iota2((tq, tq), 0)

    def body(j, carry):
        k0 = pl.multiple_of(j * tq, 8)
        dist = qpos - (k0 + _iota2((tq, tq), 1))
        bias = slope * dist.astype(F32)
        vb = v_ref[pl.ds(k0, tq), :]
        for mi, k_ref in enumerate((k0_ref, k1_ref)):
            s = _mm_nt(qs[mi], k_ref[pl.ds(k0, tq), :]) - bias
            s = jnp.where(dist >= 0, s, -jnp.inf)
            m_prev = m_s[mi]
            m_new = jnp.maximum(m_prev, jnp.max(s, axis=-1, keepdims=True))
            alpha = jnp.exp(m_prev - m_new)
            p = jnp.exp(s - m_new)
            l_s[mi] = alpha * l_s[mi] + jnp.sum(p, axis=-1, keepdims=True)
            acc_s[mi] = alpha * acc_s[mi] + _mm(p, vb)
            m_s[mi] = m_new
        return carry

    lax.fori_loop(0, qi + 1, body, 0)
    o = acc_s[0] / l_s[0] - lam_ref[0] * (acc_s[1] / l_s[1])
    on = o * lax.rsqrt(jnp.mean(o * o, axis=-1, keepdims=True) + EPS) * ng_ref[...]
    o_ref[...] = (on * out_scale).astype(o_ref.dtype)


def diff_attention_prompt(p3, lam, norm_g, out_scale, tq):
    b, t, _ = p3.shape
    qblk = lambda off: pl.BlockSpec((None, tq, B_QK_DIM), lambda bi, h, i, off=off: (bi, i, off // B_QK_DIM + h))
    kblk = lambda off: pl.BlockSpec((None, t, B_QK_DIM), lambda bi, h, i, off=off: (bi, 0, off // B_QK_DIM + h))
    kern = functools.partial(_attn_kernel, tq, out_scale)
    return pl.pallas_call(
        kern,
        grid=(b, B_HEADS, t // tq),
        in_specs=[pl.BlockSpec(memory_space=pltpu.SMEM),
                  qblk(P_BQ), qblk(P_BQ + B_HEADS * B_QK_DIM), kblk(P_BK), kblk(P_BK + B_HEADS * B_QK_DIM),
                  pl.BlockSpec((None, t, B_V_DIM), lambda bi, h, i: (bi, 0, P_BV // B_V_DIM + h)),
                  pl.BlockSpec((1, B_V_DIM), lambda bi, h, i: (0, 0))],
        out_specs=pl.BlockSpec((None, tq, B_V_DIM), lambda bi, h, i: (bi, i, h)),
        out_shape=jax.ShapeDtypeStruct((b, t, B_WIDTH), BF16),
        scratch_shapes=[pltpu.VMEM((2, tq, 1), F32), pltpu.VMEM((2, tq, 1), F32), pltpu.VMEM((2, tq, B_V_DIM), F32)],
        compiler_params=_params("parallel", "parallel", "arbitrary"),
        name="diff_attn_prompt",
    )(lam, p3, p3, p3, p3, p3, norm_g.reshape(1, B_V_DIM))


def _lanes_to_rows(x, n):
    return jnp.broadcast_to(x, (8, x.shape[1])).T[:n, 0:1]


def _attn_decode_kernel(npp, n_steps, past_len, out_scale, tbl_ref, lam_ref, qbd_ref, q_ref, kn_ref, vn_ref, ng_ref,
                        *rest):
    k_refs = rest[:npp]
    v_refs = rest[npp:2 * npp]
    o_ref = rest[2 * npp]
    m_s, l_s, acc_s = rest[2 * npp + 1:]
    step = pl.program_id(1)
    nhm = 2 * B_HEADS
    lane = _iota2((1, PAGE_SIZE), 1)
    slope = jnp.exp2(-8.0 * ((lane & (B_HEADS - 1)) + 1).astype(F32) / B_HEADS)

    @pl.when(step == 0)
    def _():
        m_s[...] = jnp.full(m_s.shape, -jnp.inf, F32)
        l_s[...] = jnp.zeros(l_s.shape, F32)
        acc_s[...] = jnp.zeros(acc_s.shape, F32)

    for pi in range(npp):
        s = jnp.dot(k_refs[pi][...].astype(BF16), qbd_ref[...], preferred_element_type=F32)
        kpos = (step * npp + pi) * PAGE_SIZE + _iota2((PAGE_SIZE, 1), 0)
        s = s - slope * (past_len - kpos).astype(F32)
        m_prev = m_s[...]
        m_new = jnp.maximum(m_prev, jnp.max(s, axis=0, keepdims=True))
        alpha = jnp.exp(m_prev - m_new)
        p = jnp.exp(s - m_new)
        l_s[...] = alpha * l_s[...] + jnp.sum(p, axis=0, keepdims=True)
        m_s[...] = m_new
        acc_s[...] = _lanes_to_rows(alpha, nhm) * acc_s[...] + _mm_tn(p[:, :nhm], v_refs[pi][...])

    @pl.when(step == n_steps - 1)
    def _():
        rowh = _iota2((nhm, B_V_DIM), 0) & (B_HEADS - 1)
        acc_all = acc_s[...]
        acc = jnp.zeros((nhm, B_V_DIM), F32)
        for h in range(B_HEADS):
            acc = jnp.where(rowh == h, acc_all[:, h * B_V_DIM:(h + 1) * B_V_DIM], acc)
        s_new = jnp.sum(q_ref[...] * (B_QK_DIM ** -0.5) * kn_ref[...], axis=-1, keepdims=True)
        m_prev = _lanes_to_rows(m_s[...], nhm)
        m_new = jnp.maximum(m_prev, s_new)
        alpha = jnp.exp(m_prev - m_new)
        p_new = jnp.exp(s_new - m_new)
        l_fin = alpha * _lanes_to_rows(l_s[...], nhm) + p_new
        vn = vn_ref[...]
        on = (alpha * acc + p_new * jnp.concatenate([vn, vn], axis=0)) / l_fin
        o = on[:B_HEADS] - lam_ref[0] * on[B_HEADS:]
        o = o * lax.rsqrt(jnp.mean(o * o, axis=-1, keepdims=True) + EPS) * ng_ref[...]
        o_ref[...] = (o * out_scale).astype(o_ref.dtype)


def diff_attention_decode(layer, q, k_new, v_new, cache_k, cache_v, page_table, lam, norm_g, out_scale, npp=8):
    b = q.shape[0]
    n_pages = page_table.shape[1]
    npp = math.gcd(n_pages, npp)
    n_steps = n_pages // npp
    nhm = 2 * B_HEADS
    kw = nhm * B_QK_DIM
    ck = cache_k.reshape(cache_k.shape[0], cache_k.shape[1], PAGE_SIZE, kw)
    cv = cache_v.reshape(cache_v.shape[0], cache_v.shape[1], PAGE_SIZE, B_WIDTH)
    qbd = ((q * (B_QK_DIM ** -0.5))[:, :, :, None] * jnp.eye(nhm, PAGE_SIZE, dtype=F32)[None, :, None, :])
    qbd = qbd.reshape(b, kw, PAGE_SIZE).astype(BF16)
    page = lambda width: (lambda pi: pl.BlockSpec((None, None, PAGE_SIZE, width),
                                                  lambda bi, s, tbl, pi=pi: (layer, tbl[bi, s * npp + pi], 0, 0)))
    kern = functools.partial(_attn_decode_kernel, npp, n_steps, n_pages * PAGE_SIZE, out_scale)
    grid_spec = pltpu.PrefetchScalarGridSpec(
        num_scalar_prefetch=1,
        grid=(b, n_steps),
        in_specs=[pl.BlockSpec(memory_space=pltpu.SMEM),
                  pl.BlockSpec((None, kw, PAGE_SIZE), lambda bi, s, tbl: (bi, 0, 0)),
                  pl.BlockSpec((None, nhm, B_QK_DIM), lambda bi, s, tbl: (bi, 0, 0)),
                  pl.BlockSpec((None, nhm, B_QK_DIM), lambda bi, s, tbl: (bi, 0, 0)),
                  pl.BlockSpec((None, B_HEADS, B_V_DIM), lambda bi, s, tbl: (bi, 0, 0)),
                  pl.BlockSpec((1, B_V_DIM), lambda bi, s, tbl: (0, 0))]
                 + [page(kw)(pi) for pi in range(npp)] + [page(B_WIDTH)(pi) for pi in range(npp)],
        out_specs=pl.BlockSpec((None, B_HEADS, B_V_DIM), lambda bi, s, tbl: (bi, 0, 0)),
        scratch_shapes=[pltpu.VMEM((1, PAGE_SIZE), F32), pltpu.VMEM((1, PAGE_SIZE), F32),
                        pltpu.VMEM((nhm, B_WIDTH), F32)],
    )
    return pl.pallas_call(
        kern,
        grid_spec=grid_spec,
        out_shape=jax.ShapeDtypeStruct((b, B_HEADS, B_V_DIM), F32),
        compiler_params=_params("parallel", "arbitrary"),
        name="diff_attn_decode",
    )(page_table, lam, qbd, q, k_new, v_new, norm_g.reshape(1, B_V_DIM), *([ck] * npp), *([cv] * npp))


def _split_in_proj(w):
    idx = [0]
    for s in IN_SIZES:
        idx.append(idx[-1] + s)
    return [w[:, idx[i]:idx[i + 1]] for i in range(len(IN_SIZES))]


def _prep_in_proj(w):
    a_qkv, a_z, a_b, a_a, b_q, b_k, b_v, c_q, c_k, c_v, c_r, c_g = _split_in_proj(w)
    d = w.shape[0]
    c_qk = jnp.concatenate([c_q.reshape(d, C_HEADS, C_DK), c_k.reshape(d, C_HEADS, C_DK)], axis=2)
    main = jnp.concatenate([a_qkv, a_z, b_q, b_k, b_v, c_qk.reshape(d, 2 * C_HEADS * C_DK), c_v, c_r], axis=1)
    pad = jnp.zeros((d, SM_WIDTH - 2 * A_HEADS - C_GATE_RANK), w.dtype)
    small = jnp.concatenate([a_b, a_a, c_g, pad], axis=1)
    return main.astype(BF16), small.astype(BF16)


def _lane_row(vals, offset):
    return jnp.zeros((1, SM_WIDTH), F32).at[0, offset:offset + vals.shape[0]].set(vals.astype(F32))


def _layer(layer, h, b, t, t_real, n_full, tm, tm_ff, states, attn_fn, prm):
    (norm1_g, w_main, w_small, a_conv_w, a_log, a_dt_bias, a_norm_g, lam, b_norm_g, w2pad, gate_b, c_norm_g,
     wo_a, wo_b, wo_c, norm2_g, w_gate, w_up, w_down) = prm
    conv_state, s_a, s_c_t = states
    u = rmsnorm(h, norm1_g, BF16, tm)
    p = matmul(u, w_main, tm, 1152)
    sm = matmul(u, w_small, tm, SM_WIDTH)
    p3 = p.reshape(b, t, P_WIDTH)
    sm3 = sm.reshape(b, t, SM_WIDTH)
    ao, new_conv, s_a_new = gdn_heads(p3, sm3, a_conv_w, conv_state, _lane_row(a_log, SM_A),
                                      _lane_row(a_dt_bias, SM_A), a_norm_g, s_a, t_real, n_full)
    out_scale = 1.0 - (0.8 - 0.6 * math.exp(-0.3 * layer))
    bo, bk, bv = attn_fn(p3, lam, b_norm_g, out_scale)
    co, s_c_t_new = gla_heads(p3, sm3, w2pad, gate_b, c_norm_g, s_c_t, t_real, n_full)
    m = b * t
    h = out_proj(ao.reshape(m, A_WIDTH), bo.reshape(m, B_WIDTH), co.reshape(m, C_WIDTH), wo_a, wo_b, wo_c, h, tm, 1024)
    u2 = rmsnorm(h, norm2_g, BF16, tm)
    act = ffn_gateup(u2, w_gate, w_up, tm_ff, 256)
    h = ffn_down(act, w_down, h, tm, 256)
    return h, new_conv, s_a_new, s_c_t_new, bk, bv


def kernel(x_prompt, x_sample, cache_k, cache_v, page_table, state_a_conv, state_a_rec, state_c_rec, meta_tokens, norm1_g, w_in, a_conv_w, a_log, a_dt_bias, a_norm_g, b_lambda_q1, b_lambda_k1, b_lambda_q2, b_lambda_k2, b_norm_g, c_gate_w2, c_gate_b, c_norm_g, w_o, norm2_g, w_gate, w_up, w_down, final_norm_g):
    bp, seq, d = x_prompt.shape
    bs = x_sample.shape[0]
    tp = seq + N_META
    ts = 8
    hp = jnp.concatenate([jnp.broadcast_to(meta_tokens[None], (bp, N_META, d)), x_prompt], axis=1).reshape(bp * tp, d)
    hs = jnp.pad(x_sample, ((0, 0), (0, ts - 1), (0, 0))).reshape(bs * ts, d)
    conv0 = jnp.zeros((bp, CONV_WIDTH - 1, 3 * A_WIDTH), F32)
    sa0 = jnp.zeros((bp, A_HEADS, A_DK, A_DV), F32)
    sc0 = jnp.zeros((bp, C_HEADS, C_DK, C_DV), F32)
    tm_p = _tile(tp * bp, 688, 16)
    tm_ff = _tile(tp * bp, 1376, 16)
    tm_s = bs * ts
    outs = [[] for _ in range(10)]
    for l in range(DEPTH):
        w_main, w_small = _prep_in_proj(w_in[l])
        lam_init = 0.8 - 0.6 * math.exp(-0.3 * l)
        lam = (jnp.exp(jnp.sum(b_lambda_q1[l] * b_lambda_k1[l])) - jnp.exp(jnp.sum(b_lambda_q2[l] * b_lambda_k2[l]))
               + lam_init).reshape(1).astype(F32)
        w2pad = jnp.zeros((C_HEADS, SM_WIDTH, C_DK), F32).at[:, SM_G:SM_G + C_GATE_RANK, :].set(
            jnp.transpose(c_gate_w2[l].reshape(C_GATE_RANK, C_HEADS, C_DK), (1, 0, 2)))
        wo = w_o[l].astype(BF16)
        prm = (norm1_g[l], w_main, w_small, a_conv_w[l], a_log[l], a_dt_bias[l], a_norm_g[l], lam, b_norm_g[l],
               w2pad, c_gate_b[l].reshape(C_HEADS, 1, C_DK), c_norm_g[l],
               wo[:A_WIDTH], wo[A_WIDTH:A_WIDTH + B_WIDTH], wo[A_WIDTH + B_WIDTH:], norm2_g[l],
               w_gate[l].astype(BF16), w_up[l].astype(BF16), w_down[l].astype(BF16))

        def attn_prompt(p3, lam, g, out_scale):
            bo = diff_attention_prompt(p3, lam, g, out_scale, tq=_tile(tp, 344, 8))
            return bo, p3[:, :, P_BK:P_BK + 2 * B_HEADS * B_QK_DIM], p3[:, :, P_BV:P_BV + B_WIDTH]

        def attn_sample(p3, lam, g, out_scale, l=l):
            row = p3[:, 0]
            q = row[:, P_BQ:P_BQ + 2 * B_HEADS * B_QK_DIM].reshape(bs, 2 * B_HEADS, B_QK_DIM)
            k_new = row[:, P_BK:P_BK + 2 * B_HEADS * B_QK_DIM].reshape(bs, 2 * B_HEADS, B_QK_DIM)
            v_new = row[:, P_BV:P_BV + B_WIDTH].reshape(bs, B_HEADS, B_V_DIM)
            bo = diff_attention_decode(l, q, k_new, v_new, cache_k, cache_v, page_table, lam, g, out_scale)
            bo = jnp.pad(bo.astype(BF16).reshape(bs, 1, B_WIDTH), ((0, 0), (0, ts - 1), (0, 0)))
            return bo, k_new, v_new

        hp, c1, a1, s1, k1, v1 = _layer(l, hp, bp, tp, tp, seq // CHUNK, tm_p, tm_ff, (conv0, sa0, sc0), attn_prompt, prm)
        st_s = (state_a_conv[l], state_a_rec[l], state_c_rec[l])
        hs, c2, a2, s2, k2, v2 = _layer(l, hs, bs, ts, 1, 0, tm_s, tm_s, st_s, attn_sample, prm)
        for lst, val in zip(outs, (k1.reshape(bp, tp, 2 * B_HEADS, B_QK_DIM), v1.reshape(bp, tp, B_HEADS, B_V_DIM),
                                   k2.reshape(bs, 1, 2 * B_HEADS, B_QK_DIM), v2.reshape(bs, 1, B_HEADS, B_V_DIM),
                                   c1, c2, a1, a2, s1, s2)):
            lst.append(val)
    y_prompt = rmsnorm(hp, final_norm_g, F32, tm_p).reshape(bp, tp, d)[:, N_META:]
    y_sample = rmsnorm(hs, final_norm_g, F32, tm_s).reshape(bs, ts, d)[:, :1]
    return (y_prompt, y_sample) + tuple(jnp.stack(o) for o in outs)
```

```python
import functools
import math

import jax
import jax.numpy as jnp
from jax import lax
from jax.experimental import pallas as pl
from jax.experimental.pallas import tpu as pltpu

F32 = jnp.float32
BF16 = jnp.bfloat16
HIGHEST = lax.Precision.HIGHEST

D_MODEL = 4096
N_META = 16
DEPTH = 4
PAGE_SIZE = 128
CHUNK = 64
SUB = 16
GLA_SUB = 16
A_HEADS, A_DK, A_DV, A_WIDTH, CONV_WIDTH = 12, 128, 128, 1536, 4
B_HEADS, B_QK_DIM, B_V_DIM, B_WIDTH = 4, 128, 256, 1024
C_HEADS, C_DK, C_DV, C_WIDTH, C_GATE_RANK, C_GATE_TAU = 4, 192, 384, 1536, 16, 16.0
D_FF = 11008
EPS = 1e-6
IN_SIZES = (3 * A_WIDTH, A_WIDTH, A_HEADS, A_HEADS, 2 * B_HEADS * B_QK_DIM, 2 * B_HEADS * B_QK_DIM, B_WIDTH,
            C_HEADS * C_DK, C_HEADS * C_DK, C_WIDTH, C_WIDTH, C_GATE_RANK)

P_QKV, P_Z, P_BQ, P_BK, P_BV, P_CQK, P_CV, P_CR, P_WIDTH = 0, 4608, 6144, 7168, 8192, 9216, 10752, 12288, 13824
SM_B, SM_A, SM_G, SM_WIDTH = 0, A_HEADS, 2 * A_HEADS, 128

VMEM_LIMIT_BYTES = 56 * 1024 * 1024


def _tile(n, cap, align):
    return max(t for t in range(align, min(n, cap) + 1, align) if n % t == 0)


def _params(*sem):
    return pltpu.CompilerParams(dimension_semantics=sem, vmem_limit_bytes=VMEM_LIMIT_BYTES)


def _sigmoid(x):
    return 1.0 / (1.0 + jnp.exp(-x))


def _silu(x):
    return x * _sigmoid(x)


def _softplus(x):
    return jnp.maximum(x, 0.0) + jnp.log1p(jnp.exp(-jnp.abs(x)))


def _mm(a, b):
    return jnp.dot(a.astype(BF16), b.astype(BF16), preferred_element_type=F32)


def _mm_nt(a, b):
    return lax.dot_general(a.astype(BF16), b.astype(BF16), (((1,), (1,)), ((), ())), preferred_element_type=F32)


def _mm_tn(a, b):
    return lax.dot_general(a.astype(BF16), b.astype(BF16), (((0,), (0,)), ((), ())), preferred_element_type=F32)


def _mm_hi(a, b):
    return jnp.dot(a, b, preferred_element_type=F32, precision=HIGHEST)


def _iota2(shape, dim):
    return lax.broadcasted_iota(jnp.int32, shape, dim)


def _bdot(a, b, ca, cb):
    return lax.dot_general(a.astype(BF16), b.astype(BF16), (((ca,), (cb,)), ((0,), (0,))), preferred_element_type=F32)


def _bmm(a, b):
    return _bdot(a, b, 2, 1)


def _bmm_nt(a, b):
    return _bdot(a, b, 2, 2)


def _bmm_tn(a, b):
    return _bdot(a, b, 1, 1)


def _split2(x):
    hi = x.astype(BF16).astype(F32)
    return hi, x - hi


def _mm3(a, b):
    ah, al = _split2(a)
    bh, bl = _split2(b)
    return _bmm(jnp.concatenate([ah, ah, al], axis=2), jnp.concatenate([bh, bl, bh], axis=1))


def _split3(x):
    p1 = x.astype(BF16).astype(F32)
    r1 = x - p1
    p2 = r1.astype(BF16).astype(F32)
    return p1, p2, r1 - p2


def _cumsum_rows(x, c):
    tri = ((_iota2((c, 3 * c), 1) & (c - 1)) <= _iota2((c, 3 * c), 0)).astype(BF16)
    return jnp.dot(tri, jnp.concatenate(_split3(x), axis=0).astype(BF16), preferred_element_type=F32)


def _rmsnorm_kernel(x_ref, g_ref, o_ref):
    x = x_ref[...]
    y = x * lax.rsqrt(jnp.mean(x * x, axis=-1, keepdims=True) + EPS)
    o_ref[...] = (y * g_ref[...]).astype(o_ref.dtype)


def rmsnorm(x, g, out_dtype, tm):
    m, d = x.shape
    return pl.pallas_call(
        _rmsnorm_kernel,
        grid=(m // tm,),
        in_specs=[pl.BlockSpec((tm, d), lambda i: (i, 0)), pl.BlockSpec((1, d), lambda i: (0, 0))],
        out_specs=pl.BlockSpec((tm, d), lambda i: (i, 0)),
        out_shape=jax.ShapeDtypeStruct((m, d), out_dtype),
        compiler_params=_params("parallel"),
        name="rmsnorm",
    )(x, g.reshape(1, d))


def _mm_kernel(x_ref, w_ref, o_ref):
    o_ref[...] = jnp.dot(x_ref[...], w_ref[...], preferred_element_type=F32).astype(o_ref.dtype)


def matmul(x, w, tm, tn, out_dtype=F32):
    m, k = x.shape
    n = w.shape[1]
    return pl.pallas_call(
        _mm_kernel,
        grid=(n // tn, m // tm),
        in_specs=[pl.BlockSpec((tm, k), lambda j, i: (i, 0)), pl.BlockSpec((k, tn), lambda j, i: (0, j))],
        out_specs=pl.BlockSpec((tm, tn), lambda j, i: (i, j)),
        out_shape=jax.ShapeDtypeStruct((m, n), out_dtype),
        compiler_params=_params("parallel", "parallel"),
        name="proj_in",
    )(x, w)


def _mm3_res_kernel(xa_ref, xb_ref, xc_ref, wa_ref, wb_ref, wc_ref, r_ref, o_ref):
    acc = jnp.dot(xa_ref[...], wa_ref[...], preferred_element_type=F32)
    acc += jnp.dot(xb_ref[...], wb_ref[...], preferred_element_type=F32)
    acc += jnp.dot(xc_ref[...], wc_ref[...], preferred_element_type=F32)
    o_ref[...] = r_ref[...] + acc


def out_proj(xa, xb, xc, wa, wb, wc, res, tm, tn):
    m = xa.shape[0]
    n = wa.shape[1]
    xs = lambda a: pl.BlockSpec((tm, a.shape[1]), lambda j, i: (i, 0))
    ws = lambda a: pl.BlockSpec((a.shape[0], tn), lambda j, i: (0, j))
    return pl.pallas_call(
        _mm3_res_kernel,
        grid=(n // tn, m // tm),
        in_specs=[xs(xa), xs(xb), xs(xc), ws(wa), ws(wb), ws(wc), pl.BlockSpec((tm, tn), lambda j, i: (i, j))],
        out_specs=pl.BlockSpec((tm, tn), lambda j, i: (i, j)),
        out_shape=jax.ShapeDtypeStruct((m, n), F32),
        compiler_params=_params("parallel", "parallel"),
        name="proj_out",
    )(xa, xb, xc, wa, wb, wc, res)


def _gateup_kernel(x_ref, wg_ref, wu_ref, o_ref):
    x = x_ref[...]
    g = jnp.dot(x, wg_ref[...], preferred_element_type=F32)
    u = jnp.dot(x, wu_ref[...], preferred_element_type=F32)
    o_ref[...] = (_silu(g) * u).astype(o_ref.dtype)


def ffn_gateup(x, wg, wu, tm, tn):
    m, k = x.shape
    n = wg.shape[1]
    return pl.pallas_call(
        _gateup_kernel,
        grid=(m // tm, n // tn),
        in_specs=[pl.BlockSpec((tm, k), lambda i, j: (i, 0)), pl.BlockSpec((k, tn), lambda i, j: (0, j)),
                  pl.BlockSpec((k, tn), lambda i, j: (0, j))],
        out_specs=pl.BlockSpec((tm, tn), lambda i, j: (i, j)),
        out_shape=jax.ShapeDtypeStruct((m, n), BF16),
        compiler_params=_params("parallel", "parallel"),
        name="ffn_gateup",
    )(x, wg, wu)


def _down_kernel(x_ref, w_ref, r_ref, o_ref):
    o_ref[...] = r_ref[...] + jnp.dot(x_ref[...], w_ref[...], preferred_element_type=F32)


def ffn_down(x, w, res, tm, tn):
    m, k = x.shape
    n = w.shape[1]
    return pl.pallas_call(
        _down_kernel,
        grid=(m // tm, n // tn),
        in_specs=[pl.BlockSpec((tm, k), lambda i, j: (i, 0)), pl.BlockSpec((k, tn), lambda i, j: (0, j)),
                  pl.BlockSpec((tm, tn), lambda i, j: (i, j))],
        out_specs=pl.BlockSpec((tm, tn), lambda i, j: (i, j)),
        out_shape=jax.ShapeDtypeStruct((m, n), F32),
        compiler_params=_params("parallel", "parallel"),
        name="ffn_down",
    )(x, w, res)


def _solve_unit_lower(a, rhs, c):
    r = _iota2((c, c), 0)
    col = _iota2((c, c), 1)
    eye = (r == col).astype(F32)
    bs = min(c, SUB)
    if c > bs:
        sh = int(math.log2(bs))
        same = (r >> sh) == (col >> sh)
        d = jnp.where(same, a, 0.0)
        e = jnp.where(same, 0.0, a)
    else:
        d = a
    m = eye - d
    p = d
    for _ in range(int(math.log2(bs)) - 1):
        p = _mm3(p, p)
        m = m + _mm3(m, p)
    y = _mm3(m, rhs)
    if c == bs:
        return y
    assert c == 4 * bs
    n = _mm3(m, e)
    y = y + _mm3(_mm3(n, n), y)
    return y - _mm3(n, y)


def _gdn_kernel(t_real, t_rows, n_full, hb, nc_main,
                q_ref, k_ref, v_ref, z_ref, sm_ref, cwq_ref, cwk_ref, cwv_ref, cs_ref, alog_ref, dtb_ref, ng_ref,
                s0_ref, ao_ref, nc_ref, sout_ref, g_s, b_s, s_s):
    grp = pl.program_id(1)
    width = hb * A_DV
    x_refs = (q_ref, k_ref, v_ref)
    for part, ref in enumerate(x_refs):
        if t_real >= CONV_WIDTH - 1:
            nc_ref[:, part, :] = ref[t_real - (CONV_WIDTH - 1):t_real, :]
        else:
            ext = jnp.concatenate([cs_ref[:, part, :], ref[0:8, :]], axis=0)
            nc_ref[:, part, :] = ext[t_real:t_real + CONV_WIDTH - 1]
    sm = sm_ref[...]
    valid = _iota2((t_rows, 1), 0) < t_real
    b_s[...] = jnp.where(valid, _sigmoid(sm), 0.0)
    g_s[...] = jnp.where(valid, -jnp.exp(alog_ref[...]) * _softplus(sm + dtb_ref[...]), 0.0)
    s_s[...] = s0_ref[...]
    lane = _iota2((1, SM_WIDTH), 1)

    sel = (_iota2((8, SM_WIDTH), 1) == SM_A + grp * hb + _iota2((8, SM_WIDTH), 0)).astype(F32)

    def group(r0, c, nc, lead_group=False):
        rows = nc * c
        r = _iota2((c, c), 0)
        col = _iota2((c, c), 1)
        gcs, grs, bts = [], [], []
        for g in range(nc):
            rr = pl.ds(r0 + g * c, c)
            gall = _cumsum_rows(g_s[rr, :], c)
            grow = sum(_mm_nt(sel, piece) for piece in _split3(gall))
            bch = b_s[rr, :]
            for i in range(hb):
                head = grp * hb + i
                gcs.append(jnp.sum(jnp.where(lane == SM_A + head, gall, 0.0), axis=-1, keepdims=True))
                bts.append(jnp.sum(jnp.where(lane == SM_B + head, bch, 0.0), axis=-1, keepdims=True))
                grs.append(grow[i:i + 1, :])
        gc, bt, gr = jnp.stack(gcs), jnp.stack(bts), jnp.stack(grs)

        def conv(part, cw_ref):
            if lead_group:
                hist = jnp.concatenate([jnp.zeros((5, width), F32), cs_ref[:, part, :]], axis=0)
                xw = jnp.concatenate([hist, x_refs[part][0:rows, :]], axis=0)
            else:
                xw = x_refs[part][pl.ds(pl.multiple_of(r0 - 8, 8), rows + 8), :]
            acc = cw_ref[0:1, :] * xw[5:5 + rows]
            for j in range(1, CONV_WIDTH):
                acc = acc + cw_ref[j:j + 1, :] * xw[5 + j:5 + j + rows]
            y = _silu(acc)
            return jnp.stack([y[g * c:(g + 1) * c, i * A_DV:(i + 1) * A_DV] for g in range(nc) for i in range(hb)])

        q = conv(0, cwq_ref)
        k = conv(1, cwk_ref)
        v = conv(2, cwv_ref)
        q = q * lax.rsqrt(jnp.sum(q * q, axis=-1, keepdims=True) + EPS) * (A_DK ** -0.5)
        k = k * lax.rsqrt(jnp.sum(k * k, axis=-1, keepdims=True) + EPS)
        decay = jnp.exp(jnp.where(r >= col, gc - gr, -jnp.inf))
        a = jnp.where(r > col, bt * _bmm_nt(k, k) * decay, 0.0)
        eg = jnp.exp(gc)
        sol = _solve_unit_lower(a, jnp.concatenate([bt * v, (bt * eg) * k], axis=2), c)
        qkd = _bmm_nt(q, k) * decay
        qe = q * eg
        gl = gc[:, c - 1:c, :]
        khat = k * jnp.exp(gl - gc)
        egl = jnp.exp(gl)
        for g in range(nc):
            sl = slice(g * hb, (g + 1) * hb)
            s = s_s[...]
            u = sol[sl, :, :A_DV] - _bmm(sol[sl, :, A_DV:], s)
            o = _bmm(qe[sl], s) + _bmm(qkd[sl], u)
            s_s[...] = egl[sl] * s + _bmm_tn(khat[sl], u)
            on = o * lax.rsqrt(jnp.mean(o * o, axis=-1, keepdims=True) + EPS) * ng_ref[...]
            rr = pl.ds(r0 + g * c, c)
            for i in range(hb):
                ls = slice(i * A_DV, (i + 1) * A_DV)
                ao_ref[rr, ls] = (on[i] * _silu(z_ref[rr, ls])).astype(ao_ref.dtype)

    lead = t_rows - n_full * CHUNK
    group(0, lead, 1, lead_group=True)
    if n_full:
        assert n_full % nc_main == 0 and lead >= 8

        def body(ci, carry):
            group(pl.multiple_of(lead + ci * (nc_main * CHUNK), SUB), CHUNK, nc_main)
            return carry
        lax.fori_loop(0, n_full // nc_main, body, 0)
    sout_ref[...] = s_s[...]


def gdn_heads(p3, sm3, conv_w, conv_state, alog, dtb, norm_g, s0, t_real, n_full, hb=2, nc_main=4):
    b, t, _ = p3.shape
    w = hb * A_DV
    ng = A_HEADS // hb
    blk = lambda off: pl.BlockSpec((None, t, w), lambda bi, g, off=off: (bi, 0, off // w + g))
    cwb = lambda part: pl.BlockSpec((CONV_WIDTH, w), lambda bi, g, part=part: (0, part * ng + g))
    row = pl.BlockSpec((1, SM_WIDTH), lambda bi, g: (0, 0))
    cs4 = conv_state.reshape(b, CONV_WIDTH - 1, 3, A_WIDTH)
    kern = functools.partial(_gdn_kernel, t_real, t, n_full, hb, math.gcd(n_full, nc_main))
    ao, nc, sout = pl.pallas_call(
        kern,
        grid=(b, ng),
        in_specs=[blk(P_QKV), blk(P_QKV + A_WIDTH), blk(P_QKV + 2 * A_WIDTH), blk(P_Z),
                  pl.BlockSpec((None, t, SM_WIDTH), lambda bi, g: (bi, 0, 0)),
                  cwb(0), cwb(1), cwb(2),
                  pl.BlockSpec((None, CONV_WIDTH - 1, 3, w), lambda bi, g: (bi, 0, 0, g)),
                  row, row, pl.BlockSpec((1, A_DV), lambda bi, g: (0, 0)),
                  pl.BlockSpec((None, hb, A_DK, A_DV), lambda bi, g: (bi, g, 0, 0))],
        out_specs=[pl.BlockSpec((None, t, w), lambda bi, g: (bi, 0, g)),
                   pl.BlockSpec((None, CONV_WIDTH - 1, 3, w), lambda bi, g: (bi, 0, 0, g)),
                   pl.BlockSpec((None, hb, A_DK, A_DV), lambda bi, g: (bi, g, 0, 0))],
        out_shape=[jax.ShapeDtypeStruct((b, t, A_WIDTH), BF16),
                   jax.ShapeDtypeStruct((b, CONV_WIDTH - 1, 3, A_WIDTH), F32),
                   jax.ShapeDtypeStruct((b, A_HEADS, A_DK, A_DV), F32)],
        scratch_shapes=[pltpu.VMEM((t, SM_WIDTH), F32), pltpu.VMEM((t, SM_WIDTH), F32),
                        pltpu.VMEM((hb, A_DK, A_DV), F32)],
        compiler_params=_params("parallel", "parallel"),
        name="gdn_heads",
    )(p3, p3, p3, p3, sm3, conv_w, conv_w, conv_w, cs4, alog, dtb, norm_g.reshape(1, A_DV), s0)
    return ao, nc.reshape(b, CONV_WIDTH - 1, 3 * A_WIDTH), sout


def _gla_kernel(t_real, t_rows, n_full, nc_main,
                qk_ref, v_ref, r_ref, sm_ref, w2_ref, gb_ref, ng_ref, s0_ref, co_ref, sout_ref, st_s, la_s):
    st_s[...] = s0_ref[...]
    valid_all = _iota2((t_rows, 1), 0) < t_real
    x = _mm_hi(sm_ref[...], w2_ref[...]) + gb_ref[...]
    la_s[...] = jnp.where(valid_all, (jnp.minimum(x, 0.0) - jnp.log1p(jnp.exp(-jnp.abs(x)))) / C_GATE_TAU, 0.0)

    def group(r0, c, nc):
        rows = nc * c
        sb = min(c, GLA_SUB)
        nsub = c // sb
        nb = nc * nsub
        tri = ((_iota2((nc, c, 3 * c), 2) & (c - 1)) <= _iota2((nc, c, 3 * c), 1))
        la = la_s[pl.ds(r0, rows), :].reshape(nc, c, C_DK)
        bc = _bmm(tri.astype(BF16), jnp.concatenate(_split3(la), axis=1))
        valid = (_iota2((rows, 1), 0) + r0) < t_real
        qk = qk_ref[pl.ds(r0, rows), :]
        q = (qk[:, :C_DK] * (C_DK ** -0.5)).reshape(nc, c, C_DK)
        k = jnp.where(valid, qk[:, C_DK:], 0.0).reshape(nc, c, C_DK)
        v = jnp.where(valid, v_ref[pl.ds(r0, rows), :], 0.0).reshape(nc, c, C_DV)
        q4, k4, b4 = (z.reshape(nb, sb, C_DK) for z in (q, k, bc))
        ls = _iota2((nb, sb, sb), 2)
        att = jnp.zeros((nb, sb, sb), F32)
        for j in range(sb):
            top = (j // 8) * 8
            e = jnp.exp(jnp.minimum(b4[:, top:] - b4[:, j:j + 1, :], 0.0))
            colj = jnp.sum(q4[:, top:] * k4[:, j:j + 1, :] * e, axis=-1, keepdims=True)
            if top:
                colj = jnp.concatenate([jnp.zeros((nb, top, 1), F32), colj], axis=1)
            att = jnp.where(ls == j, colj, att)
        att = jnp.where(_iota2((nb, sb, sb), 1) >= ls, att, 0.0)
        od = _bmm(att, v.reshape(nb, sb, C_DV)).reshape(nc, c, C_DV)
        parts = [od[:, :sb]]
        for a in range(1, nsub):
            lo = a * sb
            edge = bc[:, lo - 1:lo, :]
            qt = q[:, lo:lo + sb] * jnp.exp(bc[:, lo:lo + sb] - edge)
            kt = k[:, :lo] * jnp.exp(edge - bc[:, :lo])
            parts.append(od[:, lo:lo + sb] + _bmm(_bmm_nt(qt, kt), v[:, :lo]))
        o_intra = jnp.concatenate(parts, axis=1) if nsub > 1 else od
        bl = bc[:, c - 1:c, :]
        kv = _bmm_tn(k * jnp.exp(bl - bc), v)
        qe = q * jnp.exp(bc)
        ebl = jnp.exp(bl)
        for g in range(nc):
            st = st_s[...]
            o = _mm(qe[g], st) + o_intra[g]
            st_s[...] = jnp.broadcast_to(ebl[g], (8, C_DK)).T[:, 0:1] * st + kv[g]
            rr = pl.ds(r0 + g * c, c)
            on = o * lax.rsqrt(jnp.mean(o * o, axis=-1, keepdims=True) + EPS) * ng_ref[...]
            co_ref[rr, :] = (on * _silu(r_ref[rr, :])).astype(co_ref.dtype)

    lead = t_rows - n_full * CHUNK
    group(0, lead, 1)
    if n_full:
        assert n_full % nc_main == 0

        def body(ci, carry):
            group(pl.multiple_of(lead + ci * (nc_main * CHUNK), SUB), CHUNK, nc_main)
            return carry
        lax.fori_loop(0, n_full // nc_main, body, 0)
    sout_ref[...] = st_s[...]


def gla_heads(p3, sm3, w2pad, gate_b, norm_g, s0, t_real, n_full, nc_main=4):
    b, t, _ = p3.shape
    blk = lambda off: pl.BlockSpec((None, t, C_DV), lambda bi, h, off=off: (bi, 0, off // C_DV + h))
    kern = functools.partial(_gla_kernel, t_real, t, n_full, math.gcd(n_full, nc_main))
    return pl.pallas_call(
        kern,
        grid=(b, C_HEADS),
        in_specs=[blk(P_CQK), blk(P_CV), blk(P_CR),
                  pl.BlockSpec((None, t, SM_WIDTH), lambda bi, h: (bi, 0, 0)),
                  pl.BlockSpec((None, SM_WIDTH, C_DK), lambda bi, h: (h, 0, 0)),
                  pl.BlockSpec((None, 1, C_DK), lambda bi, h: (h, 0, 0)),
                  pl.BlockSpec((1, C_DV), lambda bi, h: (0, 0)),
                  pl.BlockSpec((None, None, C_DK, C_DV), lambda bi, h: (bi, h, 0, 0))],
        out_specs=[pl.BlockSpec((None, t, C_DV), lambda bi, h: (bi, 0, h)),
                   pl.BlockSpec((None, None, C_DK, C_DV), lambda bi, h: (bi, h, 0, 0))],
        out_shape=[jax.ShapeDtypeStruct((b, t, C_WIDTH), BF16),
                   jax.ShapeDtypeStruct((b, C_HEADS, C_DK, C_DV), F32)],
        scratch_shapes=[pltpu.VMEM((C_DK, C_DV), F32), pltpu.VMEM((t, C_DK), F32)],
        compiler_params=_params("parallel", "parallel"),
        name="gla_heads",
    )(p3, p3, p3, sm3, w2pad, gate_b, norm_g.reshape(1, C_DV), s0)


def _attn_kernel(tq, out_scale, lam_ref, q0_ref, q1_ref, k0_ref, k1_ref, v_ref, ng_ref, o_ref,
                 m_s, l_s, acc_s):
    head = pl.program_id(1)
    qi = pl.program_id(2)
    slope = jnp.exp2(jnp.full((1, 1), -8.0 / B_HEADS, F32) * (head + 1).astype(F32))
    scale = B_QK_DIM ** -0.5
    qs = (q0_ref[...] * scale, q1_ref[...] * scale)
    m_s[...] = jnp.full(m_s.shape, -jnp.inf, F32)
    l_s[...] = jnp.zeros(l_s.shape, F32)
    acc_s[...] = jnp.zeros(acc_s.shape, F32)
    qpos = qi * tq + _iota2((tq, tq), 0)

    def body(j, carry):
        k0 = pl.multiple_of(j * tq, 8)
        dist = qpos - (k0 + _iota2((tq, tq), 1))
        bias = slope * dist.astype(F32)
        vb = v_ref[pl.ds(k0, tq), :]
        for mi, k_ref in enumerate((k0_ref, k1_ref)):
            s = _mm_nt(qs[mi], k_ref[pl.ds(k0, tq), :]) - bias
            s = jnp.where(dist >= 0, s, -jnp.inf)
            m_prev = m_s[mi]
            m_new = jnp.maximum(m_prev, jnp.max(s, axis=-1, keepdims=True))
            alpha = jnp.exp(m_prev - m_new)
            p = jnp.exp(s - m_new)
            l_s[mi] = alpha * l_s[mi] + jnp.sum(p, axis=-1, keepdims=True)
            acc_s[mi] = alpha * acc_s[mi] + _mm(p, vb)
            m_s[mi] = m_new
        return carry

    lax.fori_loop(0, qi + 1, body, 0)
    o = acc_s[0] / l_s[0] - lam_ref[0] * (acc_s[1] / l_s[1])
    on = o * lax.rsqrt(jnp.mean(o * o, axis=-1, keepdims=True) + EPS) * ng_ref[...]
    o_ref[...] = (on * out_scale).astype(o_ref.dtype)


def diff_attention_prompt(p3, lam, norm_g, out_scale, tq):
    b, t, _ = p3.shape
    qblk = lambda off: pl.BlockSpec((None, tq, B_QK_DIM), lambda bi, h, i, off=off: (bi, i, off // B_QK_DIM + h))
    kblk = lambda off: pl.BlockSpec((None, t, B_QK_DIM), lambda bi, h, i, off=off: (bi, 0, off // B_QK_DIM + h))
    kern = functools.partial(_attn_kernel, tq, out_scale)
    return pl.pallas_call(
        kern,
        grid=(b, B_HEADS, t // tq),
        in_specs=[pl.BlockSpec(memory_space=pltpu.SMEM),
                  qblk(P_BQ), qblk(P_BQ + B_HEADS * B_QK_DIM), kblk(P_BK), kblk(P_BK + B_HEADS * B_QK_DIM),
                  pl.BlockSpec((None, t, B_V_DIM), lambda bi, h, i: (bi, 0, P_BV // B_V_DIM + h)),
                  pl.BlockSpec((1, B_V_DIM), lambda bi, h, i: (0, 0))],
        out_specs=pl.BlockSpec((None, tq, B_V_DIM), lambda bi, h, i: (bi, i, h)),
        out_shape=jax.ShapeDtypeStruct((b, t, B_WIDTH), BF16),
        scratch_shapes=[pltpu.VMEM((2, tq, 1), F32), pltpu.VMEM((2, tq, 1), F32), pltpu.VMEM((2, tq, B_V_DIM), F32)],
        compiler_params=_params("parallel", "parallel", "arbitrary"),
        name="diff_attn_prompt",
    )(lam, p3, p3, p3, p3, p3, norm_g.reshape(1, B_V_DIM))


def _attn_decode_kernel(npp, n_steps, past_len, out_scale, tbl_ref, lam_ref, q_ref, kn_ref, vn_ref, ng_ref, *rest):
    k_refs = rest[:npp]
    v_refs = rest[npp:2 * npp]
    o_ref = rest[2 * npp]
    m_s, l_s, acc_s = rest[2 * npp + 1:]
    step = pl.program_id(1)
    nhm = 2 * B_HEADS
    row = _iota2((nhm, 1), 0)
    slope = jnp.exp2(-8.0 * ((row & (B_HEADS - 1)) + 1).astype(F32) / B_HEADS)
    q = q_ref[...] * (B_QK_DIM ** -0.5)

    @pl.when(step == 0)
    def _():
        m_s[...] = jnp.full(m_s.shape, -jnp.inf, F32)
        l_s[...] = jnp.zeros(l_s.shape, F32)
        acc_s[...] = jnp.zeros(acc_s.shape, F32)

    for pi in range(npp):
        s = jnp.sum(k_refs[pi][...] * q[None], axis=-1, keepdims=True)
        kpos = (step * npp + pi) * PAGE_SIZE + _iota2((PAGE_SIZE, 1, 1), 0)
        s = s - slope[None] * (past_len - kpos).astype(F32)
        m_prev = m_s[...]
        m_new = jnp.maximum(m_prev, jnp.max(s, axis=0))
        alpha = jnp.exp(m_prev - m_new)
        p = jnp.exp(s - m_new[None])
        l_s[...] = alpha * l_s[...] + jnp.sum(p, axis=0)
        m_s[...] = m_new
        v3 = v_refs[pi][...]
        for mi in range(2):
            rows = slice(mi * B_HEADS, (mi + 1) * B_HEADS)
            acc_s[mi] = alpha[rows] * acc_s[mi] + jnp.sum(p[:, rows, :] * v3, axis=0)

    @pl.when(step == n_steps - 1)
    def _():
        s_new = jnp.sum(q * kn_ref[...], axis=-1, keepdims=True)
        m_prev = m_s[...]
        m_new = jnp.maximum(m_prev, s_new)
        alpha = jnp.exp(m_prev - m_new)
        p_new = jnp.exp(s_new - m_new)
        l_fin = alpha * l_s[...] + p_new
        vn = vn_ref[...]
        acc = jnp.concatenate([acc_s[0], acc_s[1]], axis=0)
        on = (alpha * acc + p_new * jnp.concatenate([vn, vn], axis=0)) / l_fin
        o = on[:B_HEADS] - lam_ref[0] * on[B_HEADS:]
        o = o * lax.rsqrt(jnp.mean(o * o, axis=-1, keepdims=True) + EPS) * ng_ref[...]
        o_ref[...] = (o * out_scale).astype(o_ref.dtype)


def diff_attention_decode(layer, q, k_new, v_new, cache_k, cache_v, page_table, lam, norm_g, out_scale, npp=8):
    b = q.shape[0]
    n_pages = page_table.shape[1]
    npp = math.gcd(n_pages, npp)
    n_steps = n_pages // npp
    nhm = 2 * B_HEADS
    page = lambda heads, dim: (lambda pi: pl.BlockSpec((None, None, PAGE_SIZE, heads, dim),
                                                       lambda bi, s, tbl, pi=pi: (layer, tbl[bi, s * npp + pi], 0, 0, 0)))
    kern = functools.partial(_attn_decode_kernel, npp, n_steps, n_pages * PAGE_SIZE, out_scale)
    grid_spec = pltpu.PrefetchScalarGridSpec(
        num_scalar_prefetch=1,
        grid=(b, n_steps),
        in_specs=[pl.BlockSpec(memory_space=pltpu.SMEM),
                  pl.BlockSpec((None, nhm, B_QK_DIM), lambda bi, s, tbl: (bi, 0, 0)),
                  pl.BlockSpec((None, nhm, B_QK_DIM), lambda bi, s, tbl: (bi, 0, 0)),
                  pl.BlockSpec((None, B_HEADS, B_V_DIM), lambda bi, s, tbl: (bi, 0, 0)),
                  pl.BlockSpec((1, B_V_DIM), lambda bi, s, tbl: (0, 0))]
                 + [page(nhm, B_QK_DIM)(pi) for pi in range(npp)] + [page(B_HEADS, B_V_DIM)(pi) for pi in range(npp)],
        out_specs=pl.BlockSpec((None, B_HEADS, B_V_DIM), lambda bi, s, tbl: (bi, 0, 0)),
        scratch_shapes=[pltpu.VMEM((nhm, 1), F32), pltpu.VMEM((nhm, 1), F32),
                        pltpu.VMEM((2, B_HEADS, B_V_DIM), F32)],
    )
    return pl.pallas_call(
        kern,
        grid_spec=grid_spec,
        out_shape=jax.ShapeDtypeStruct((b, B_HEADS, B_V_DIM), F32),
        compiler_params=_params("parallel", "arbitrary"),
        name="diff_attn_decode",
    )(page_table, lam, q, k_new, v_new, norm_g.reshape(1, B_V_DIM), *([cache_k] * npp), *([cache_v] * npp))


def _split_in_proj(w):
    idx = [0]
    for s in IN_SIZES:
        idx.append(idx[-1] + s)
    return [w[:, idx[i]:idx[i + 1]] for i in range(len(IN_SIZES))]


def _prep_in_proj(w):
    a_qkv, a_z, a_b, a_a, b_q, b_k, b_v, c_q, c_k, c_v, c_r, c_g = _split_in_proj(w)
    d = w.shape[0]
    c_qk = []
    for h in range(C_HEADS):
        c_qk += [c_q[:, h * C_DK:(h + 1) * C_DK], c_k[:, h * C_DK:(h + 1) * C_DK]]
    main = jnp.concatenate([a_qkv, a_z, b_q, b_k, b_v] + c_qk + [c_v, c_r], axis=1)
    pad = jnp.zeros((d, SM_WIDTH - 2 * A_HEADS - C_GATE_RANK), w.dtype)
    small = jnp.concatenate([a_b, a_a, c_g, pad], axis=1)
    return main.astype(BF16), small.astype(BF16)


def _lane_row(vals, offset):
    return jnp.zeros((1, SM_WIDTH), F32).at[0, offset:offset + vals.shape[0]].set(vals.astype(F32))


def _layer(layer, h, b, t, t_real, n_full, tm, tm_ff, states, attn_fn, prm):
    (norm1_g, w_main, w_small, a_conv_w, a_log, a_dt_bias, a_norm_g, lam, b_norm_g, w2pad, gate_b, c_norm_g,
     wo_a, wo_b, wo_c, norm2_g, w_gate, w_up, w_down) = prm
    conv_state, s_a, s_c_t = states
    u = rmsnorm(h, norm1_g, BF16, tm)
    p = matmul(u, w_main, tm, 1152)
    sm = matmul(u, w_small, tm, SM_WIDTH)
    p3 = p.reshape(b, t, P_WIDTH)
    sm3 = sm.reshape(b, t, SM_WIDTH)
    ao, new_conv, s_a_new = gdn_heads(p3, sm3, a_conv_w, conv_state, _lane_row(a_log, SM_A),
                                      _lane_row(a_dt_bias, SM_A), a_norm_g, s_a, t_real, n_full)
    out_scale = 1.0 - (0.8 - 0.6 * math.exp(-0.3 * layer))
    bo, bk, bv = attn_fn(p3, lam, b_norm_g, out_scale)
    co, s_c_t_new = gla_heads(p3, sm3, w2pad, gate_b, c_norm_g, s_c_t, t_real, n_full)
    m = b * t
    h = out_proj(ao.reshape(m, A_WIDTH), bo.reshape(m, B_WIDTH), co.reshape(m, C_WIDTH), wo_a, wo_b, wo_c, h, tm, 1024)
    u2 = rmsnorm(h, norm2_g, BF16, tm)
    act = ffn_gateup(u2, w_gate, w_up, tm_ff, 256)
    h = ffn_down(act, w_down, h, tm, 256)
    return h, new_conv, s_a_new, s_c_t_new, bk, bv


def kernel(x_prompt, x_sample, cache_k, cache_v, page_table, state_a_conv, state_a_rec, state_c_rec, meta_tokens, norm1_g, w_in, a_conv_w, a_log, a_dt_bias, a_norm_g, b_lambda_q1, b_lambda_k1, b_lambda_q2, b_lambda_k2, b_norm_g, c_gate_w2, c_gate_b, c_norm_g, w_o, norm2_g, w_gate, w_up, w_down, final_norm_g):
    bp, seq, d = x_prompt.shape
    bs = x_sample.shape[0]
    tp = seq + N_META
    ts = 8
    hp = jnp.concatenate([jnp.broadcast_to(meta_tokens[None], (bp, N_META, d)), x_prompt], axis=1).reshape(bp * tp, d)
    hs = jnp.pad(x_sample, ((0, 0), (0, ts - 1), (0, 0))).reshape(bs * ts, d)
    conv0 = jnp.zeros((bp, CONV_WIDTH - 1, 3 * A_WIDTH), F32)
    sa0 = jnp.zeros((bp, A_HEADS, A_DK, A_DV), F32)
    sc0 = jnp.zeros((bp, C_HEADS, C_DK, C_DV), F32)
    tm_p = _tile(tp * bp, 688, 16)
    tm_ff = _tile(tp * bp, 1376, 16)
    tm_s = bs * ts
    outs = [[] for _ in range(10)]
    for l in range(DEPTH):
        w_main, w_small = _prep_in_proj(w_in[l])
        lam_init = 0.8 - 0.6 * math.exp(-0.3 * l)
        lam = (jnp.exp(jnp.sum(b_lambda_q1[l] * b_lambda_k1[l])) - jnp.exp(jnp.sum(b_lambda_q2[l] * b_lambda_k2[l]))
               + lam_init).reshape(1).astype(F32)
        w2pad = jnp.zeros((C_HEADS, SM_WIDTH, C_DK), F32).at[:, SM_G:SM_G + C_GATE_RANK, :].set(
            jnp.transpose(c_gate_w2[l].reshape(C_GATE_RANK, C_HEADS, C_DK), (1, 0, 2)))
        wo = w_o[l].astype(BF16)
        prm = (norm1_g[l], w_main, w_small, a_conv_w[l], a_log[l], a_dt_bias[l], a_norm_g[l], lam, b_norm_g[l],
               w2pad, c_gate_b[l].reshape(C_HEADS, 1, C_DK), c_norm_g[l],
               wo[:A_WIDTH], wo[A_WIDTH:A_WIDTH + B_WIDTH], wo[A_WIDTH + B_WIDTH:], norm2_g[l],
               w_gate[l].astype(BF16), w_up[l].astype(BF16), w_down[l].astype(BF16))

        def attn_prompt(p3, lam, g, out_scale):
            bo = diff_attention_prompt(p3, lam, g, out_scale, tq=_tile(tp, 344, 8))
            return bo, p3[:, :, P_BK:P_BK + 2 * B_HEADS * B_QK_DIM], p3[:, :, P_BV:P_BV + B_WIDTH]

        def attn_sample(p3, lam, g, out_scale, l=l):
            row = p3[:, 0]
            q = row[:, P_BQ:P_BQ + 2 * B_HEADS * B_QK_DIM].reshape(bs, 2 * B_HEADS, B_QK_DIM)
            k_new = row[:, P_BK:P_BK + 2 * B_HEADS * B_QK_DIM].reshape(bs, 2 * B_HEADS, B_QK_DIM)
            v_new = row[:, P_BV:P_BV + B_WIDTH].reshape(bs, B_HEADS, B_V_DIM)
            bo = diff_attention_decode(l, q, k_new, v_new, cache_k, cache_v, page_table, lam, g, out_scale)
            bo = jnp.pad(bo.astype(BF16).reshape(bs, 1, B_WIDTH), ((0, 0), (0, ts - 1), (0, 0)))
            return bo, k_new, v_new

        hp, c1, a1, s1, k1, v1 = _layer(l, hp, bp, tp, tp, seq // CHUNK, tm_p, tm_ff, (conv0, sa0, sc0), attn_prompt, prm)
        st_s = (state_a_conv[l], state_a_rec[l], state_c_rec[l])
        hs, c2, a2, s2, k2, v2 = _layer(l, hs, bs, ts, 1, 0, tm_s, tm_s, st_s, attn_sample, prm)
        for lst, val in zip(outs, (k1.reshape(bp, tp, 2 * B_HEADS, B_QK_DIM), v1.reshape(bp, tp, B_HEADS, B_V_DIM),
                                   k2.reshape(bs, 1, 2 * B_HEADS, B_QK_DIM), v2.reshape(bs, 1, B_HEADS, B_V_DIM),
                                   c1, c2, a1, a2, s1, s2)):
            lst.append(val)
    y_prompt = rmsnorm(hp, final_norm_g, F32, tm_p).reshape(bp, tp, d)[:, N_META:]
    y_sample = rmsnorm(hs, final_norm_g, F32, tm_s).reshape(bs, ts, d)[:, :1]
    return (y_prompt, y_sample) + tuple(jnp.stack(o) for o in outs)
```
